```python
import math
import jax, jax.numpy as jnp
from jax import lax
import numpy as np

D_MODEL = 2048
BATCH = 2
SEQ = 4096
DEPTH = 2

N_MIXERS = 2
HEAD_DIM = 64
ROPE_DIM = HEAD_DIM // 4
ROPE_THETA = 500000.0
ATTN_BLOCK = 128
SWA_Q_HEADS = 32
SWA_KV_HEADS = 4
SWA_WINDOW = 128
NSA_Q_HEADS = 32
NSA_KV_HEADS = 4
CMP_BLOCK = 32
CMP_STRIDE = 16
SEL_BLOCK = 64
SEL_TOPK = 16
NSA_WINDOW = 512
CMP_HIDDEN = 256
NSA_QCHUNK = 64
FORCE_BONUS = 1e4
N_GROUPS = 4
EXPERTS_PER_GROUP = 4
N_EXPERTS = N_GROUPS * EXPERTS_PER_GROUP
TOPK_IN_GROUP = 2
D_EXPERT = D_MODEL // 4
ALPHA = (2 * DEPTH) ** 0.25
BETA = (8 * DEPTH) ** -0.25
LN_EPS = 1e-5
NEG_INF = -1e30
N_SWA_LAYERS = (DEPTH + 1) // 2
N_NSA_LAYERS = DEPTH // 2
SWA_QKV_COLS = (SWA_Q_HEADS + 2 * SWA_KV_HEADS) * HEAD_DIM
NSA_IN_COLS = NSA_Q_HEADS * HEAD_DIM + 6 * NSA_KV_HEADS * HEAD_DIM + 3 * NSA_Q_HEADS

kernel_name = "hybrid_swa_nsa_hmoe_deepnorm_adaln"


def layer_norm(x, g, b):
    xf = x.astype(jnp.float32)
    mu = jnp.mean(xf, -1, keepdims=True)
    var = jnp.mean(jnp.square(xf - mu), -1, keepdims=True)
    y = (xf - mu) * lax.rsqrt(var + LN_EPS)
    return (y * g.astype(jnp.float32) + b.astype(jnp.float32)).astype(x.dtype)


def rope_tables(positions):
    inv_freq = ROPE_THETA ** (-jnp.arange(0, ROPE_DIM, 2, dtype=jnp.float32) / ROPE_DIM)
    ang = positions.astype(jnp.float32)[..., None] * inv_freq
    return jnp.cos(ang)[:, :, None, :], jnp.sin(ang)[:, :, None, :]


def apply_partial_rope(x, cos, sin):
    half = ROPE_DIM // 2
    x1 = x[..., :half].astype(jnp.float32)
    x2 = x[..., half:ROPE_DIM].astype(jnp.float32)
    rot = jnp.concatenate([x1 * cos - x2 * sin, x2 * cos + x1 * sin], -1).astype(x.dtype)
    return jnp.concatenate([rot, x[..., ROPE_DIM:]], -1)


def banded_attention(q, k, v, window, sinks=None):
    B, S, Hkv, G, D = q.shape
    n_blocks = S // ATTN_BLOCK
    halo = -(-window // ATTN_BLOCK) * ATTN_BLOCK
    span = ATTN_BLOCK + halo
    pad = ((0, 0), (halo, 0), (0, 0), (0, 0))
    kp = jnp.pad(k, pad)
    vp = jnp.pad(v, pad)
    scale = D ** -0.5

    def one_block(n):
        start = n * ATTN_BLOCK
        qb = lax.dynamic_slice_in_dim(q, start, ATTN_BLOCK, axis=1)
        kb = lax.dynamic_slice_in_dim(kp, start, span, axis=1)
        vb = lax.dynamic_slice_in_dim(vp, start, span, axis=1)
        s = jnp.einsum('bqhgd,bkhd->bhgqk', qb, kb).astype(jnp.float32) * scale
        qpos = start + jnp.arange(ATTN_BLOCK)
        kpos = start - halo + jnp.arange(span)
        rel = qpos[:, None] - kpos[None, :]
        mask = (rel >= 0) & (rel < window) & (kpos[None, :] >= 0)
        s = jnp.where(mask, s, NEG_INF)
        if sinks is not None:
            sink = jnp.broadcast_to(sinks.astype(jnp.float32)[None, :, :, None, None], s.shape[:-1] + (1,))
            p = jax.nn.softmax(jnp.concatenate([s, sink], -1), -1)[..., :-1]
        else:
            p = jax.nn.softmax(s, -1)
        return jnp.einsum('bhgqk,bkhd->bqhgd', p.astype(vb.dtype), vb)

    out = lax.map(one_block, jnp.arange(n_blocks))
    return jnp.moveaxis(out, 0, 1).reshape(B, S, Hkv, G, D)


def swa_mixer(h, cos, sin, w_qkv, b_qkv, sinks, w_o):
    B, S, _ = h.shape
    G = SWA_Q_HEADS // SWA_KV_HEADS
    qkv = h @ w_qkv + b_qkv
    q, k, v = jnp.split(qkv, [SWA_Q_HEADS * HEAD_DIM, (SWA_Q_HEADS + SWA_KV_HEADS) * HEAD_DIM], -1)
    q = apply_partial_rope(q.reshape(B, S, SWA_Q_HEADS, HEAD_DIM), cos, sin)
    k = apply_partial_rope(k.reshape(B, S, SWA_KV_HEADS, HEAD_DIM), cos, sin)
    v = v.reshape(B, S, SWA_KV_HEADS, HEAD_DIM)
    o = banded_attention(q.reshape(B, S, SWA_KV_HEADS, G, HEAD_DIM), k, v, SWA_WINDOW,
                         sinks.reshape(SWA_KV_HEADS, G))
    return o.reshape(B, S, SWA_Q_HEADS * HEAD_DIM) @ w_o


def compress_blocks(x, pe, w1, w2):
    B, S, H, D = x.shape
    chunks = x.reshape(B, S // CMP_STRIDE, CMP_STRIDE, H, D)
    blocks = jnp.concatenate([chunks[:, :-1], chunks[:, 1:]], axis=2)
    blocks = blocks + pe[None, None, :, None, :]
    n_cmp = blocks.shape[1]
    flat = jnp.moveaxis(blocks, 3, 2).reshape(B, n_cmp, H, CMP_BLOCK * D)
    return jax.nn.silu(flat @ w1) @ w2


def cmp_sel_overlap(n_cmp, n_sel):
    c0 = jnp.arange(n_cmp)[:, None] * CMP_STRIDE
    s0 = jnp.arange(n_sel)[None, :] * SEL_BLOCK
    ov = jnp.clip(jnp.minimum(c0 + CMP_BLOCK, s0 + SEL_BLOCK) - jnp.maximum(c0, s0), 0)
    return ov.astype(jnp.float32) / CMP_BLOCK


def nsa_mixer(h, cos, sin, w_in, pe_k, pe_v, phi_k1, phi_k2, phi_v1, phi_v2, w_o):
    B, S, _ = h.shape
    HQ, HKV, D = NSA_Q_HEADS, NSA_KV_HEADS, HEAD_DIM
    G = HQ // HKV
    kvw = HKV * D
    proj = h @ w_in
    q, k_c, v_c, k_s, v_s, k_w, v_w, gates = jnp.split(proj, [HQ * D + i * kvw for i in range(7)], -1)
    q = q.reshape(B, S, HQ, D)
    q_raw = q.reshape(B, S, HKV, G, D)
    q_rot = apply_partial_rope(q, cos, sin).reshape(B, S, HKV, G, D)
    shp = (B, S, HKV, D)
    k_s = apply_partial_rope(k_s.reshape(shp), cos, sin)
    k_w = apply_partial_rope(k_w.reshape(shp), cos, sin)
    v_s = v_s.reshape(shp)
    v_w = v_w.reshape(shp)
    kc = compress_blocks(k_c.reshape(shp), pe_k, phi_k1, phi_k2)
    vc = compress_blocks(v_c.reshape(shp), pe_v, phi_v1, phi_v2)
    n_cmp = kc.shape[1]
    n_sel = S // SEL_BLOCK
    top_k = min(SEL_TOPK, n_sel)
    ks_blocks = k_s.reshape(B, n_sel, SEL_BLOCK, HKV, D).transpose(0, 3, 1, 2, 4)
    vs_blocks = v_s.reshape(B, n_sel, SEL_BLOCK, HKV, D).transpose(0, 3, 1, 2, 4)
    overlap = cmp_sel_overlap(n_cmp, n_sel)
    cmp_end = jnp.arange(n_cmp) * CMP_STRIDE + CMP_BLOCK - 1
    sel_idx = jnp.arange(n_sel)
    scale = D ** -0.5
    bi = jnp.arange(B)[:, None, None, None]
    hi = jnp.arange(HKV)[None, :, None, None]

    def one_chunk(n):
        start = n * NSA_QCHUNK
        t = start + jnp.arange(NSA_QCHUNK)
        qc = lax.dynamic_slice_in_dim(q_raw, start, NSA_QCHUNK, axis=1)
        qr = lax.dynamic_slice_in_dim(q_rot, start, NSA_QCHUNK, axis=1)
        s = jnp.einsum('bthgd,bnhd->bhgtn', qc, kc).astype(jnp.float32) * scale
        valid = cmp_end[None, :] <= t[:, None]
        p = jax.nn.softmax(jnp.where(valid, s, NEG_INF), -1) * valid
        o_cmp = jnp.einsum('bhgtn,bnhd->bthgd', p.astype(vc.dtype), vc)
        imp = jnp.einsum('bhgtn,ns->bhts', p, overlap)
        cur = t[:, None] // SEL_BLOCK
        causal = sel_idx[None, :] * SEL_BLOCK <= t[:, None]
        forced = (sel_idx[None, :] == 0) | (sel_idx[None, :] == cur) | (sel_idx[None, :] == cur - 1)
        score = jnp.where(causal, imp + jnp.where(forced, FORCE_BONUS, 0.0), NEG_INF)
        _, idx = lax.top_k(score, top_k)
        ksel = ks_blocks[bi, hi, idx].reshape(B, HKV, NSA_QCHUNK, top_k * SEL_BLOCK, D)
        vsel = vs_blocks[bi, hi, idx].reshape(B, HKV, NSA_QCHUNK, top_k * SEL_BLOCK, D)
        kpos = (idx[..., None] * SEL_BLOCK + jnp.arange(SEL_BLOCK)).reshape(B, HKV, NSA_QCHUNK, top_k * SEL_BLOCK)
        m2 = (kpos <= t[:, None])[:, :, None]
        s2 = jnp.einsum('bthgd,bhtkd->bhgtk', qr, ksel).astype(jnp.float32) * scale
        p2 = jax.nn.softmax(jnp.where(m2, s2, NEG_INF), -1)
        o_sel = jnp.einsum('bhgtk,bhtkd->bthgd', p2.astype(vsel.dtype), vsel)
        return o_cmp, o_sel

    o_cmp, o_sel = lax.map(one_chunk, jnp.arange(S // NSA_QCHUNK))
    o_cmp = jnp.moveaxis(o_cmp, 0, 1).reshape(B, S, HKV, G, D)
    o_sel = jnp.moveaxis(o_sel, 0, 1).reshape(B, S, HKV, G, D)
    o_win = banded_attention(q_rot, k_w, v_w, NSA_WINDOW)
    g = jax.nn.sigmoid(gates.astype(jnp.float32)).reshape(B, S, HKV, G, 3).astype(h.dtype)
    o = g[..., 0:1] * o_cmp + g[..., 1:2] * o_sel + g[..., 2:3] * o_win
    return o.reshape(B, S, HQ * D) @ w_o


def hier_moe(h, w_group, b_group, w_router, b_router, w1, w3, w2):
    B, S, Dm = h.shape
    T = B * S
    t = h.reshape(T, Dm)
    pg = jax.nn.softmax((t @ w_group + b_group).astype(jnp.float32), -1)
    g_prob, g_idx = lax.top_k(pg, 1)
    le = (t @ w_router + b_router).astype(jnp.float32).reshape(T, N_GROUPS, EXPERTS_PER_GROUP)
    g_onehot = jax.nn.one_hot(g_idx[:, 0], N_GROUPS, dtype=jnp.float32)
    le = jnp.sum(g_onehot[:, :, None] * le, axis=1)
    pe = jax.nn.softmax(le, -1)
    e_prob, e_idx = lax.top_k(pe, TOPK_IN_GROUP)
    e_prob = e_prob / jnp.sum(e_prob, -1, keepdims=True)
    expert_id = g_idx * EXPERTS_PER_GROUP + e_idx
    weight = g_prob * e_prob
    combine = jnp.sum(jax.nn.one_hot(expert_id, N_EXPERTS, dtype=jnp.float32) * weight[..., None], axis=1)
    combine = combine.astype(t.dtype)
    y = jnp.zeros_like(t)
    for e in range(N_EXPERTS):
        he = jax.nn.silu(t @ w1[e]) * (t @ w3[e])
        y = y + combine[:, e:e + 1] * (he @ w2[e])
    return y.reshape(B, S, Dm)


def setup_inputs(seed: int = 0) -> dict:
    key = jax.random.key(seed)
    ks = jax.random.split(key, 32)
    f32 = jnp.float32
    D = D_MODEL
    nrm = lambda k, shape, s: jax.random.normal(k, shape, f32) * s
    x = nrm(ks[0], (BATCH, SEQ, D), 1.0)
    c = nrm(ks[1], (BATCH, D), 1.0)
    positions = jnp.broadcast_to(jnp.arange(SEQ, dtype=jnp.int32)[None, :], (BATCH, SEQ))
    w_ada = nrm(ks[2], (DEPTH, D, 6 * D), D ** -0.5)
    b_ada = nrm(ks[3], (DEPTH, 6 * D), 0.01)
    swa_scale = jnp.concatenate([jnp.ones(((SWA_Q_HEADS + SWA_KV_HEADS) * HEAD_DIM,), f32),
                                 jnp.full((SWA_KV_HEADS * HEAD_DIM,), BETA, f32)])
    swa_w_qkv = nrm(ks[4], (N_SWA_LAYERS, D, SWA_QKV_COLS), D ** -0.5) * swa_scale
    swa_b_qkv = nrm(ks[5], (N_SWA_LAYERS, SWA_QKV_COLS), 0.01)
    swa_sinks = nrm(ks[6], (N_SWA_LAYERS, SWA_Q_HEADS), 1.0)
    swa_w_o = nrm(ks[7], (N_SWA_LAYERS, SWA_Q_HEADS * HEAD_DIM, D), (SWA_Q_HEADS * HEAD_DIM) ** -0.5 * BETA)
    kvw = NSA_KV_HEADS * HEAD_DIM
    ones_kv = jnp.ones((kvw,), f32)
    beta_kv = jnp.full((kvw,), BETA, f32)
    nsa_scale = jnp.concatenate([jnp.ones((NSA_Q_HEADS * HEAD_DIM,), f32),
                                 ones_kv, beta_kv, ones_kv, beta_kv, ones_kv, beta_kv,
                                 jnp.ones((3 * NSA_Q_HEADS,), f32)])
    nsa_w_in = nrm(ks[8], (N_NSA_LAYERS, D, NSA_IN_COLS), D ** -0.5) * nsa_scale
    nsa_pe_k = nrm(ks[9], (N_NSA_LAYERS, CMP_BLOCK, HEAD_DIM), 0.1)
    nsa_pe_v = nrm(ks[10], (N_NSA_LAYERS, CMP_BLOCK, HEAD_DIM), 0.1)
    fan = CMP_BLOCK * HEAD_DIM
    nsa_phi_k1 = nrm(ks[11], (N_NSA_LAYERS, fan, CMP_HIDDEN), fan ** -0.5)
    nsa_phi_k2 = nrm(ks[12], (N_NSA_LAYERS, CMP_HIDDEN, HEAD_DIM), CMP_HIDDEN ** -0.5)
    nsa_phi_v1 = nrm(ks[13], (N_NSA_LAYERS, fan, CMP_HIDDEN), fan ** -0.5)
    nsa_phi_v2 = nrm(ks[14], (N_NSA_LAYERS, CMP_HIDDEN, HEAD_DIM), CMP_HIDDEN ** -0.5)
    nsa_w_o = nrm(ks[15], (N_NSA_LAYERS, NSA_Q_HEADS * HEAD_DIM, D), (NSA_Q_HEADS * HEAD_DIM) ** -0.5 * BETA)
    moe_w_group = nrm(ks[16], (DEPTH, D, N_GROUPS), D ** -0.5)
    moe_b_group = nrm(ks[17], (DEPTH, N_GROUPS), 0.01)
    moe_w_router = nrm(ks[18], (DEPTH, D, N_EXPERTS), D ** -0.5)
    moe_b_router = nrm(ks[19], (DEPTH, N_EXPERTS), 0.01)
    moe_w1 = nrm(ks[20], (DEPTH, N_EXPERTS, D, D_EXPERT), D ** -0.5)
    moe_w3 = nrm(ks[21], (DEPTH, N_EXPERTS, D, D_EXPERT), D ** -0.5)
    moe_w2 = nrm(ks[22], (DEPTH, N_EXPERTS, D_EXPERT, D), D_EXPERT ** -0.5 * BETA)
    ln_t_g = 1.0 + nrm(ks[23], (DEPTH, D), 0.01)
    ln_t_b = nrm(ks[24], (DEPTH, D), 0.01)
    ln_c_g = 1.0 + nrm(ks[25], (DEPTH, D), 0.01)
    ln_c_b = nrm(ks[26], (DEPTH, D), 0.01)
    return {"x": x, "c": c, "positions": positions, "w_ada": w_ada, "b_ada": b_ada,
            "swa_w_qkv": swa_w_qkv, "swa_b_qkv": swa_b_qkv, "swa_sinks": swa_sinks, "swa_w_o": swa_w_o,
            "nsa_w_in": nsa_w_in, "nsa_pe_k": nsa_pe_k, "nsa_pe_v": nsa_pe_v,
            "nsa_phi_k1": nsa_phi_k1, "nsa_phi_k2": nsa_phi_k2, "nsa_phi_v1": nsa_phi_v1, "nsa_phi_v2": nsa_phi_v2,
            "nsa_w_o": nsa_w_o,
            "moe_w_group": moe_w_group, "moe_b_group": moe_b_group,
            "moe_w_router": moe_w_router, "moe_b_router": moe_b_router,
            "moe_w1": moe_w1, "moe_w3": moe_w3, "moe_w2": moe_w2,
            "ln_t_g": ln_t_g, "ln_t_b": ln_t_b, "ln_c_g": ln_c_g, "ln_c_b": ln_c_b}


def reference(x, c, positions, w_ada, b_ada, swa_w_qkv, swa_b_qkv, swa_sinks, swa_w_o,
              nsa_w_in, nsa_pe_k, nsa_pe_v, nsa_phi_k1, nsa_phi_k2, nsa_phi_v1, nsa_phi_v2, nsa_w_o,
              moe_w_group, moe_b_group, moe_w_router, moe_b_router, moe_w1, moe_w3, moe_w2,
              ln_t_g, ln_t_b, ln_c_g, ln_c_b):
    cos, sin = rope_tables(positions)
    c_act = jax.nn.silu(c)
    for i in range(DEPTH):
        mod = (c_act @ w_ada[i] + b_ada[i])[:, None, :]
        sh_t, sc_t, g_t, sh_c, sc_c, g_c = jnp.split(mod, 6, -1)
        h = x * (1.0 + sc_t) + sh_t
        j = i // N_MIXERS
        if i % N_MIXERS == 0:
            out = swa_mixer(h, cos, sin, swa_w_qkv[j], swa_b_qkv[j], swa_sinks[j], swa_w_o[j])
        else:
            out = nsa_mixer(h, cos, sin, nsa_w_in[j], nsa_pe_k[j], nsa_pe_v[j], nsa_phi_k1[j], nsa_phi_k2[j],
                            nsa_phi_v1[j], nsa_phi_v2[j], nsa_w_o[j])
        x = layer_norm(ALPHA * x + g_t * out, ln_t_g[i], ln_t_b[i])
        h = x * (1.0 + sc_c) + sh_c
        out = hier_moe(h, moe_w_group[i], moe_b_group[i], moe_w_router[i], moe_b_router[i],
                       moe_w1[i], moe_w3[i], moe_w2[i])
        x = layer_norm(ALPHA * x + g_c * out, ln_c_g[i], ln_c_b[i])
    return x
```

```python
import functools

import jax
import jax.numpy as jnp
from jax import lax
from jax.experimental import pallas as pl
from jax.experimental.pallas import tpu as pltpu

F32 = jnp.float32
BF16 = jnp.bfloat16

HEAD_DIM = 64
ROPE_DIM = HEAD_DIM // 4
ROPE_HALF = ROPE_DIM // 2
ROPE_THETA = 500000.0
Q_HEADS = 32
KV_HEADS = 4
GROUP = Q_HEADS // KV_HEADS
SWA_WINDOW = 128
NSA_WINDOW = 512
CMP_BLOCK = 32
CMP_STRIDE = 16
SEL_BLOCK = 64
SEL_TOPK = 16
FORCE_BONUS = 1e4
N_GROUPS = 4
EXPERTS_PER_GROUP = 4
N_EXPERTS = N_GROUPS * EXPERTS_PER_GROUP
TOPK_IN_GROUP = 2
DEPTH = 2
ALPHA = (2 * DEPTH) ** 0.25
LN_EPS = 1e-5
NEG_INF = -1e30
PICKED = -3e38

LANES = 128
ATTN_BLOCK = 128
VMEM_LIMIT = 56 * 1024 * 1024


def _cparams(*sem):
    return pltpu.CompilerParams(dimension_semantics=sem, vmem_limit_bytes=VMEM_LIMIT)


def _mod_kernel(cb_ref, w_ref, b_ref, o_ref, cs_ref, *, n_batch, tn):
    @pl.when((pl.program_id(0) == 0) & (pl.program_id(1) == 0))
    def _():
        c = cb_ref[...]
        cs_ref[...] = c * (1.0 / (1.0 + jnp.exp(-c)))

    for b in range(n_batch):
        cs = cs_ref[b]
        parts = []
        for g in range(tn // LANES):
            wg = w_ref[0, :, g * LANES:(g + 1) * LANES]
            parts.append(jnp.sum(wg * cs, axis=0, keepdims=True))
        o_ref[0, b:b + 1, :] = jnp.concatenate(parts, axis=-1) + b_ref[0]


def _adaln_mod(c, w_ada, b_ada):
    n_batch, d = c.shape
    depth, _, n = w_ada.shape
    tn = 512
    cb = jnp.broadcast_to(c[:, :, None], (n_batch, d, LANES))
    return pl.pallas_call(
        functools.partial(_mod_kernel, n_batch=n_batch, tn=tn),
        grid=(depth, n // tn),
        in_specs=[
            pl.BlockSpec((n_batch, d, LANES), lambda l, j: (0, 0, 0)),
            pl.BlockSpec((1, d, tn), lambda l, j: (l, 0, j)),
            pl.BlockSpec((1, 1, tn), lambda l, j: (l, 0, j)),
        ],
        out_specs=pl.BlockSpec((1, n_batch, tn), lambda l, j: (l, 0, j)),
        out_shape=jax.ShapeDtypeStruct((depth, n_batch, n), F32),
        scratch_shapes=[pltpu.VMEM((n_batch, d, LANES), F32)],
        compiler_params=_cparams("arbitrary", "arbitrary"),
    )(cb, w_ada, b_ada.reshape(depth, 1, n))


def _rope_table_kernel(pos_ref, inv_ref, c_ref, s1_ref, s2_ref):
    ang = pos_ref[...] * inv_ref[...]
    d = lax.broadcasted_iota(jnp.int32, ang.shape, 1) % HEAD_DIM
    cos = jnp.cos(ang)
    sin = jnp.sin(ang)
    c_ref[...] = jnp.where(d < ROPE_DIM, cos, 1.0)
    s1_ref[...] = jnp.where(d < ROPE_HALF, -sin, 0.0)
    s2_ref[...] = jnp.where((d >= ROPE_HALF) & (d < ROPE_DIM), sin, 0.0)


def _rope_tables(positions):
    t = positions.size
    tm = 1024
    inv_freq = ROPE_THETA ** (-jnp.arange(0, ROPE_DIM, 2, dtype=F32) / ROPE_DIM)
    lane = jnp.arange(LANES) % HEAD_DIM
    inv_lane = inv_freq[lane % ROPE_HALF].reshape(1, LANES)
    pos = positions.astype(F32).reshape(t, 1)
    spec = pl.BlockSpec((tm, LANES), lambda i: (i, 0))
    return pl.pallas_call(
        _rope_table_kernel,
        grid=(t // tm,),
        in_specs=[pl.BlockSpec((tm, 1), lambda i: (i, 0)),
                  pl.BlockSpec((1, LANES), lambda i: (0, 0))],
        out_specs=[spec, spec, spec],
        out_shape=[jax.ShapeDtypeStruct((t, LANES), F32)] * 3,
        compiler_params=_cparams("arbitrary"),
    )(pos, inv_lane)


def _apply_rope(x, c, s1, s2):
    reps = x.shape[-1] // LANES
    if reps > 1:
        c = jnp.concatenate([c] * reps, axis=-1)
        s1 = jnp.concatenate([s1] * reps, axis=-1)
        s2 = jnp.concatenate([s2] * reps, axis=-1)
    n = x.shape[-1]
    up = pltpu.roll(x, n - ROPE_HALF, 1)
    down = pltpu.roll(x, ROPE_HALF, 1)
    return x * c + up * s1 + down * s2


def _proj_kernel(x_ref, sc_ref, sh_ref, w_ref, b_ref, c_ref, s1_ref, s2_ref, o_ref, h_ref,
                 *, rope_lo, rope_hi):
    j = pl.program_id(1)

    @pl.when(j == 0)
    def _():
        h_ref[...] = (x_ref[...] * (1.0 + sc_ref[0]) + sh_ref[0]).astype(BF16)

    acc = jnp.dot(h_ref[...], w_ref[...], preferred_element_type=F32) + b_ref[...]
    if rope_hi > rope_lo:
        is_rope = (j >= rope_lo) & (j < rope_hi)

        @pl.when(is_rope)
        def _():
            o_ref[...] = _apply_rope(acc, c_ref[...], s1_ref[...], s2_ref[...]).astype(o_ref.dtype)

        @pl.when(jnp.logical_not(is_rope))
        def _():
            o_ref[...] = acc.astype(o_ref.dtype)
    else:
        o_ref[...] = acc.astype(o_ref.dtype)


def _mod_proj(x, scale, shift, w, bias, tables, seq, *, rope_lo=0, rope_hi=0, tn=256, tm=1024,
              out_dtype=BF16):
    t, d = x.shape
    n = w.shape[1]
    n_batch = scale.shape[0]
    tm = min(tm, seq)
    per_batch = seq // tm
    c_tab, s1_tab, s2_tab = tables
    vec = pl.BlockSpec((1, 1, d), lambda i, j: (i // per_batch, 0, 0))
    tab = pl.BlockSpec((tm, LANES), lambda i, j: (i, 0))
    return pl.pallas_call(
        functools.partial(_proj_kernel, rope_lo=rope_lo, rope_hi=rope_hi),
        grid=(t // tm, n // tn),
        in_specs=[
            pl.BlockSpec((tm, d), lambda i, j: (i, 0)),
            vec, vec,
            pl.BlockSpec((d, tn), lambda i, j: (0, j)),
            pl.BlockSpec((1, tn), lambda i, j: (0, j)),
            tab, tab, tab,
        ],
        out_specs=pl.BlockSpec((tm, tn), lambda i, j: (i, j)),
        out_shape=jax.ShapeDtypeStruct((t, n), out_dtype),
        scratch_shapes=[pltpu.VMEM((tm, d), BF16)],
        compiler_params=_cparams("arbitrary", "arbitrary"),
    )(x, scale.reshape(n_batch, 1, d), shift.reshape(n_batch, 1, d), w, bias.reshape(1, n),
      c_tab, s1_tab, s2_tab)


def _stack_heads(q, kv_head):
    base = kv_head * GROUP
    return jnp.concatenate(
        [q[:, (base + g) * HEAD_DIM:(base + g + 1) * HEAD_DIM] for g in range(GROUP)], axis=0)


def _unstack_heads(o, rows):
    return jnp.concatenate([o[g * rows:(g + 1) * rows] for g in range(GROUP)], axis=-1)


def _rope_scale_q(q_ref, c_ref, s1_ref, s2_ref):
    q = q_ref[...].astype(F32)
    q = _apply_rope(q, c_ref[...], s1_ref[...], s2_ref[...])
    return (q * (HEAD_DIM ** -0.5)).astype(BF16)


def _band_attn_kernel(*refs, window, use_sinks, use_gate, gate_col):
    it = iter(refs)
    q_ref, k_ref, v_ref, c_ref, s1_ref, s2_ref = (next(it) for _ in range(6))
    sink_ref = next(it) if use_sinks else None
    gate_ref = next(it) if use_gate else None
    o_ref = next(it)

    n = pl.program_id(1)
    rows = ATTN_BLOCK
    halo = -(-window // rows) * rows
    span = rows + halo
    start = jnp.maximum(n * rows - halo, 0)
    start = pl.multiple_of(start, rows)
    q = _rope_scale_q(q_ref, c_ref, s1_ref, s2_ref)
    k = k_ref[pl.ds(start, span), :]
    v = v_ref[pl.ds(start, span), :]
    qpos = n * rows + lax.broadcasted_iota(jnp.int32, (rows, span), 0)
    kpos = start + lax.broadcasted_iota(jnp.int32, (rows, span), 1)
    rel = qpos - kpos
    mask = (rel >= 0) & (rel < window)
    if use_gate:
        gate = 1.0 / (1.0 + jnp.exp(-gate_ref[...].astype(F32)))

    for j in range(KV_HEADS):
        kj = k[:, j * HEAD_DIM:(j + 1) * HEAD_DIM]
        vj = v[:, j * HEAD_DIM:(j + 1) * HEAD_DIM]
        q8 = _stack_heads(q, j)
        s = lax.dot_general(q8, kj, (((1,), (1,)), ((), ())), preferred_element_type=F32)
        s = s.reshape(GROUP, rows, span)
        s = jnp.where(mask[None], s, NEG_INF)
        m = jnp.max(s, axis=-1, keepdims=True)
        if use_sinks:
            sink = jnp.concatenate(
                [jnp.full((1, 1, 1), sink_ref[j * GROUP + g], F32) for g in range(GROUP)], axis=0)
            m = jnp.maximum(m, sink)
        e = jnp.exp(s - m)
        den = jnp.sum(e, axis=-1, keepdims=True)
        if use_sinks:
            den = den + jnp.exp(sink - m)
        p = (e / den).astype(BF16).reshape(GROUP * rows, span)
        o = jnp.dot(p, vj, preferred_element_type=F32)
        if use_gate:
            o = o.reshape(GROUP, rows, HEAD_DIM)
            o = jnp.concatenate(
                [o[g] * gate[:, gate_col + j * GROUP + g:gate_col + j * GROUP + g + 1]
                 for g in range(GROUP)], axis=-1)
        else:
            o = _unstack_heads(o, rows)
        o_ref[:, j * GROUP * HEAD_DIM:(j + 1) * GROUP * HEAD_DIM] = o.astype(o_ref.dtype)


def _band_attention(q, k, v, tables, seq, window, sinks=None, gates=None, gate_col=0):
    t = q.shape[0]
    n_batch = t // seq
    nb = seq // ATTN_BLOCK
    kvw = KV_HEADS * HEAD_DIM
    row = lambda w: pl.BlockSpec((ATTN_BLOCK, w), lambda b, i: (b * nb + i, 0))
    res = pl.BlockSpec((seq, kvw), lambda b, i: (b, 0))
    in_specs = [row(q.shape[1]), res, res, row(LANES), row(LANES), row(LANES)]
    args = [q, k, v, *tables]
    if sinks is not None:
        in_specs.append(pl.BlockSpec(memory_space=pltpu.SMEM))
        args.append(sinks)
    if gates is not None:
        in_specs.append(row(LANES))
        args.append(gates)
    return pl.pallas_call(
        functools.partial(_band_attn_kernel, window=window, use_sinks=sinks is not None,
                          use_gate=gates is not None, gate_col=gate_col),
        grid=(n_batch, nb),
        in_specs=in_specs,
        out_specs=row(q.shape[1]),
        out_shape=jax.ShapeDtypeStruct(q.shape, BF16),
        compiler_params=_cparams("arbitrary", "arbitrary"),
    )(*args)


def _layer_norm(y, g, b):
    mu = jnp.mean(y, axis=-1, keepdims=True)
    yc = y - mu
    var = jnp.mean(yc * yc, axis=-1, keepdims=True)
    return yc * lax.rsqrt(var + LN_EPS) * g + b


def _out_proj_kernel(*refs, n_parts):
    o_parts = refs[:n_parts]
    w_ref, x_ref, gate_ref, g_ref, b_ref, out_ref = refs[n_parts:]
    if n_parts == 1:
        o = o_parts[0][...]
    else:
        acc = o_parts[0][...].astype(F32)
        for r in o_parts[1:]:
            acc = acc + r[...].astype(F32)
        o = acc.astype(BF16)
    mix = jnp.dot(o, w_ref[...], preferred_element_type=F32)
    y = ALPHA * x_ref[...] + gate_ref[0] * mix
    out_ref[...] = _layer_norm(y, g_ref[...], b_ref[...])


def _out_proj_ln(o_parts, w_o, x, gate, ln_g, ln_b, seq, tm=256):
    t, d = x.shape
    n_batch = gate.shape[0]
    per_batch = seq // tm
    k = w_o.shape[0]
    row_o = pl.BlockSpec((tm, k), lambda i: (i, 0))
    row_x = pl.BlockSpec((tm, d), lambda i: (i, 0))
    vec = pl.BlockSpec((1, d), lambda i: (0, 0))
    return pl.pallas_call(
        functools.partial(_out_proj_kernel, n_parts=len(o_parts)),
        grid=(t // tm,),
        in_specs=[row_o] * len(o_parts) + [
            pl.BlockSpec((k, d), lambda i: (0, 0)),
            row_x,
            pl.BlockSpec((1, 1, d), lambda i: (i // per_batch, 0, 0)),
            vec, vec],
        out_specs=row_x,
        out_shape=jax.ShapeDtypeStruct((t, d), F32),
        compiler_params=_cparams("arbitrary"),
    )(*o_parts, w_o, x, gate.reshape(n_batch, 1, d), ln_g.reshape(1, d), ln_b.reshape(1, d))


def _compress_kernel(x_ref, pe_ref, w1_ref, w1a_ref, w1b_ref, w2_ref, o_ref):
    x = x_ref[0]
    a = jnp.dot(x, w1a_ref[...], preferred_element_type=F32)
    b = jnp.dot(x, w1b_ref[...], preferred_element_type=F32)
    nc = a.shape[0]
    b_next = pltpu.roll(b, nc - 1, 0)
    pe_term = jnp.dot(pe_ref[...], w1_ref[...], preferred_element_type=F32)[0:1]
    hid = a + b_next + jnp.concatenate([pe_term] * KV_HEADS, axis=-1)
    hid = hid * (1.0 / (1.0 + jnp.exp(-hid)))
    o_ref[0] = jnp.dot(hid.astype(BF16), w2_ref[...], preferred_element_type=F32).astype(o_ref.dtype)


def _block_diag_heads(w):
    p, d, n = w.shape
    eye = jnp.eye(KV_HEADS, dtype=w.dtype)
    big = w[:, None, :, None, :] * eye[None, :, None, :, None]
    return big.reshape(p * KV_HEADS * d, KV_HEADS * n)


def _compress(xc, pe, w1, w2, seq):
    t, kvw = xc.shape
    n_batch = t // seq
    nch = seq // CMP_STRIDE
    hidden = w1.shape[1]
    x = xc.reshape(n_batch, nch, CMP_STRIDE * kvw)
    w1r = w1.reshape(CMP_BLOCK, HEAD_DIM, hidden)
    w1a = _block_diag_heads(w1r[:CMP_STRIDE]).astype(BF16)
    w1b = _block_diag_heads(w1r[CMP_STRIDE:]).astype(BF16)
    eye = jnp.eye(KV_HEADS, dtype=w2.dtype)
    w2d = (w2[None, :, None, :] * eye[:, None, :, None]).reshape(KV_HEADS * hidden, kvw).astype(BF16)
    pe8 = jnp.broadcast_to(pe.reshape(1, CMP_BLOCK * HEAD_DIM), (8, CMP_BLOCK * HEAD_DIM)).astype(BF16)
    full = lambda a: pl.BlockSpec(a.shape, lambda b: (0,) * a.ndim)
    w1b16 = w1.astype(BF16)
    return pl.pallas_call(
        _compress_kernel,
        grid=(n_batch,),
        in_specs=[pl.BlockSpec((1, nch, CMP_STRIDE * kvw), lambda b: (b, 0, 0)),
                  full(pe8), full(w1b16), full(w1a), full(w1b), full(w2d)],
        out_specs=pl.BlockSpec((1, nch, kvw), lambda b: (b, 0, 0)),
        out_shape=jax.ShapeDtypeStruct((n_batch, nch, kvw), BF16),
        compiler_params=_cparams("arbitrary"),
    )(x, pe8, w1b16, w1a, w1b, w2d)


def _cmp_attn_kernel(q_ref, kc_ref, vc_ref, ov_ref, gate_ref, o_ref, sel_ref, *, gate_col, top_k):
    n = pl.program_id(1)
    rows = ATTN_BLOCK
    nc = kc_ref.shape[1]
    nsel = ov_ref.shape[1]
    q = (q_ref[...].astype(F32) * (HEAD_DIM ** -0.5)).astype(BF16)
    kc = kc_ref[0]
    vc = vc_ref[0]
    t = n * rows + lax.broadcasted_iota(jnp.int32, (rows, nc), 0)
    cmp_end = lax.broadcasted_iota(jnp.int32, (rows, nc), 1) * CMP_STRIDE + (CMP_BLOCK - 1)
    valid = cmp_end <= t
    validf = valid.astype(F32)
    gate = 1.0 / (1.0 + jnp.exp(-gate_ref[...].astype(F32)))

    ts = n * rows + lax.broadcasted_iota(jnp.int32, (rows, nsel), 0)
    blk = lax.broadcasted_iota(jnp.int32, (rows, nsel), 1)
    cur = ts // SEL_BLOCK
    causal = blk * SEL_BLOCK <= ts
    forced = (blk == 0) | (blk == cur) | (blk == cur - 1)
    bonus = jnp.where(forced, FORCE_BONUS, 0.0)

    for j in range(KV_HEADS):
        kj = kc[:, j * HEAD_DIM:(j + 1) * HEAD_DIM]
        vj = vc[:, j * HEAD_DIM:(j + 1) * HEAD_DIM]
        q8 = _stack_heads(q, j)
        s = lax.dot_general(q8, kj, (((1,), (1,)), ((), ())), preferred_element_type=F32)
        s = jnp.where(valid[None], s.reshape(GROUP, rows, nc), NEG_INF)
        m = jnp.max(s, axis=-1, keepdims=True)
        e = jnp.exp(s - m)
        p = e / jnp.sum(e, axis=-1, keepdims=True) * validf[None]
        o = jnp.dot(p.astype(BF16).reshape(GROUP * rows, nc), vj, preferred_element_type=F32)
        o = o.reshape(GROUP, rows, HEAD_DIM)
        o = jnp.concatenate(
            [o[g] * gate[:, gate_col + j * GROUP + g:gate_col + j * GROUP + g + 1]
             for g in range(GROUP)], axis=-1)
        o_ref[:, j * GROUP * HEAD_DIM:(j + 1) * GROUP * HEAD_DIM] = o.astype(o_ref.dtype)

        psum = jnp.sum(p, axis=0)
        imp = jnp.dot(psum, ov_ref[...], preferred_element_type=F32,
                      precision=lax.Precision.HIGHEST)
        score = jnp.where(causal, imp + bonus, NEG_INF)
        chosen = jnp.zeros(score.shape, dtype=jnp.bool_)
        for _ in range(top_k):
            best = jnp.max(score, axis=-1, keepdims=True)
            first = jnp.min(jnp.where(score == best, blk, nsel), axis=-1, keepdims=True)
            pick = blk == first
            chosen = chosen | pick
            score = jnp.where(pick, PICKED, score)
        sel_ref[0, j] = jnp.where(chosen, 0.0, NEG_INF).astype(sel_ref.dtype)


def _cmp_attention(q, kc, vc, gates, seq, gate_col):
    t = q.shape[0]
    n_batch = t // seq
    nb = seq // ATTN_BLOCK
    nc = kc.shape[1]
    nsel = seq // SEL_BLOCK
    top_k = min(SEL_TOPK, nsel)
    c0 = jnp.arange(nc)[:, None] * CMP_STRIDE
    s0 = jnp.arange(nsel)[None, :] * SEL_BLOCK
    ov = jnp.clip(jnp.minimum(c0 + CMP_BLOCK, s0 + SEL_BLOCK) - jnp.maximum(c0, s0), 0)
    overlap = ov.astype(F32) / CMP_BLOCK
    row = lambda w: pl.BlockSpec((ATTN_BLOCK, w), lambda b, i: (b * nb + i, 0))
    res = pl.BlockSpec((1, nc, kc.shape[2]), lambda b, i: (b, 0, 0))
    return pl.pallas_call(
        functools.partial(_cmp_attn_kernel, gate_col=gate_col, top_k=top_k),
        grid=(n_batch, nb),
        in_specs=[row(q.shape[1]), res, res,
                  pl.BlockSpec((nc, nsel), lambda b, i: (0, 0)), row(LANES)],
        out_specs=[row(q.shape[1]),
                   pl.BlockSpec((1, KV_HEADS, ATTN_BLOCK, nsel), lambda b, i: (b, 0, i, 0))],
        out_shape=[jax.ShapeDtypeStruct(q.shape, BF16),
                   jax.ShapeDtypeStruct((n_batch, KV_HEADS, seq, nsel), BF16)],
        compiler_params=_cparams("arbitrary", "arbitrary"),
    )(q, kc, vc, overlap, gates)


def _sel_attn_kernel(q_ref, sel_ref, ka_ref, v_ref, c_ref, s1_ref, s2_ref, gate_ref, o_ref,
                     m_ref, l_ref, acc_ref, *, gate_col):
    j = pl.program_id(1)
    n = pl.program_id(2)
    rows = ATTN_BLOCK
    tk = ATTN_BLOCK
    q = _rope_scale_q(q_ref, c_ref, s1_ref, s2_ref)
    bias = sel_ref[0, 0]
    nsel = bias.shape[-1]
    qa = jnp.concatenate(
        [jnp.concatenate([q[:, g * HEAD_DIM:(g + 1) * HEAD_DIM], bias], axis=-1)
         for g in range(GROUP)], axis=0)

    def tile(kt, diag):
        ka = ka_ref[0, 0, pl.ds(pl.multiple_of(kt * tk, tk), tk), :]
        v = v_ref[0, 0, pl.ds(pl.multiple_of(kt * tk, tk), tk), :]
        s = lax.dot_general(qa, ka, (((1,), (1,)), ((), ())), preferred_element_type=F32)
        if diag:
            r = lax.broadcasted_iota(jnp.int32, (rows, tk), 0)
            c = lax.broadcasted_iota(jnp.int32, (rows, tk), 1)
            s = jnp.where((c <= r)[None], s.reshape(GROUP, rows, tk), NEG_INF)
            s = s.reshape(GROUP * rows, tk)
        m_old = m_ref[...]
        m_new = jnp.maximum(m_old, jnp.max(s, axis=-1, keepdims=True))
        alpha = jnp.exp(m_old - m_new)
        p = jnp.exp(s - m_new)
        l_ref[...] = l_ref[...] * alpha + jnp.sum(p, axis=-1, keepdims=True)
        acc_ref[...] = acc_ref[...] * alpha + jnp.dot(p.astype(BF16), v, preferred_element_type=F32)
        m_ref[...] = m_new

    m_ref[...] = jnp.full(m_ref.shape, NEG_INF, F32)
    l_ref[...] = jnp.zeros(l_ref.shape, F32)
    acc_ref[...] = jnp.zeros(acc_ref.shape, F32)

    def body(kt, carry):
        tile(kt, False)
        return carry

    lax.fori_loop(0, n, body, 0)
    tile(n, True)

    gate = 1.0 / (1.0 + jnp.exp(-gate_ref[...].astype(F32)))
    lane = lax.broadcasted_iota(jnp.int32, gate.shape, 1)
    o = (acc_ref[...] / l_ref[...]).reshape(GROUP, rows, HEAD_DIM)
    outs = []
    for g in range(GROUP):
        col = gate_col + j * GROUP + g
        gcol = jnp.sum(jnp.where(lane == col, gate, 0.0), axis=-1, keepdims=True)
        outs.append(o[g] * gcol)
    o_ref[...] = jnp.concatenate(outs, axis=-1).astype(o_ref.dtype)


def _sel_attention(q, selb, k_aug, v_heads, tables, gates, seq, gate_col):
    t = q.shape[0]
    n_batch = t // seq
    nb = seq // ATTN_BLOCK
    nsel = selb.shape[-1]
    gw = GROUP * HEAD_DIM
    row = lambda w: pl.BlockSpec((ATTN_BLOCK, w), lambda b, j, i: (b * nb + i, 0))
    return pl.pallas_call(
        functools.partial(_sel_attn_kernel, gate_col=gate_col),
        grid=(n_batch, KV_HEADS, nb),
        in_specs=[
            pl.BlockSpec((ATTN_BLOCK, gw), lambda b, j, i: (b * nb + i, j)),
            pl.BlockSpec((1, 1, ATTN_BLOCK, nsel), lambda b, j, i: (b, j, i, 0)),
            pl.BlockSpec((1, 1, seq, HEAD_DIM + nsel), lambda b, j, i: (b, j, 0, 0)),
            pl.BlockSpec((1, 1, seq, HEAD_DIM), lambda b, j, i: (b, j, 0, 0)),
            row(LANES), row(LANES), row(LANES), row(LANES)],
        out_specs=pl.BlockSpec((ATTN_BLOCK, gw), lambda b, j, i: (b * nb + i, j)),
        out_shape=jax.ShapeDtypeStruct(q.shape, BF16),
        scratch_shapes=[pltpu.VMEM((GROUP * ATTN_BLOCK, 1), F32),
                        pltpu.VMEM((GROUP * ATTN_BLOCK, 1), F32),
                        pltpu.VMEM((GROUP * ATTN_BLOCK, HEAD_DIM), F32)],
        compiler_params=_cparams("arbitrary", "arbitrary", "arbitrary"),
    )(q, selb, k_aug, v_heads, *tables, gates)


def _router_kernel(x_ref, sc_ref, sh_ref, w_ref, b_ref, o_ref):
    h = x_ref[...] * (1.0 + sc_ref[0]) + sh_ref[0]
    logits = jnp.dot(h, w_ref[...], preferred_element_type=F32,
                     precision=lax.Precision.HIGHEST) + b_ref[...]
    lane = lax.broadcasted_iota(jnp.int32, logits.shape, 1)
    big = 1 << 20

    gmask = lane < N_GROUPS
    lg = jnp.where(gmask, logits, -jnp.inf)
    eg = jnp.exp(lg - jnp.max(lg, axis=-1, keepdims=True))
    pg = eg / jnp.sum(eg, axis=-1, keepdims=True)
    g_prob = jnp.max(pg, axis=-1, keepdims=True)
    g_idx = jnp.min(jnp.where((pg == g_prob) & gmask, lane, big), axis=-1, keepdims=True)

    lo = N_GROUPS + g_idx * EXPERTS_PER_GROUP
    emask = (lane >= lo) & (lane < lo + EXPERTS_PER_GROUP)
    le = jnp.where(emask, logits, -jnp.inf)
    ee = jnp.exp(le - jnp.max(le, axis=-1, keepdims=True))
    pe = jnp.where(emask, ee / jnp.sum(ee, axis=-1, keepdims=True), -1.0)
    p1 = jnp.max(pe, axis=-1, keepdims=True)
    i1 = jnp.min(jnp.where(pe == p1, lane, big), axis=-1, keepdims=True)
    pe2 = jnp.where(lane == i1, -1.0, pe)
    p2 = jnp.max(pe2, axis=-1, keepdims=True)
    i2 = jnp.min(jnp.where(pe2 == p2, lane, big), axis=-1, keepdims=True)
    tot = p1 + p2
    w1 = g_prob * (p1 / tot)
    w2 = g_prob * (p2 / tot)
    e1 = (i1 - N_GROUPS).astype(F32)
    e2 = (i2 - N_GROUPS).astype(F32)
    o_ref[...] = jnp.where(lane == 0, e1, jnp.where(lane == 1, e2,
                           jnp.where(lane == 2, w1, jnp.where(lane == 3, w2, 0.0))))


def _router(x, scale, shift, w_group, b_group, w_router, b_router, seq, tm=512):
    t, d = x.shape
    n_batch = scale.shape[0]
    per_batch = seq // tm
    w = jnp.zeros((d, LANES), F32).at[:, :N_GROUPS].set(w_group)
    w = w.at[:, N_GROUPS:N_GROUPS + N_EXPERTS].set(w_router)
    b = jnp.zeros((1, LANES), F32).at[0, :N_GROUPS].set(b_group)
    b = b.at[0, N_GROUPS:N_GROUPS + N_EXPERTS].set(b_router)
    vec = pl.BlockSpec((1, 1, d), lambda i: (i // per_batch, 0, 0))
    return pl.pallas_call(
        _router_kernel,
        grid=(t // tm,),
        in_specs=[pl.BlockSpec((tm, d), lambda i: (i, 0)), vec, vec,
                  pl.BlockSpec((d, LANES), lambda i: (0, 0)),
                  pl.BlockSpec((1, LANES), lambda i: (0, 0))],
        out_specs=pl.BlockSpec((tm, LANES), lambda i: (i, 0)),
        out_shape=jax.ShapeDtypeStruct((t, LANES), F32),
        compiler_params=_cparams("arbitrary"),
    )(x, scale.reshape(n_batch, 1, d), shift.reshape(n_batch, 1, d), w, b)


MOE_TILE = 256


def _expert_kernel(te_ref, src_ref, nt_ref, x_hbm, rowb_ref, sc_ref, sh_ref, w1_ref, w3_ref, w2_ref,
                   o_ref, xbuf, sem, *, n_batch):
    i = pl.program_id(0)
    n_used = nt_ref[0]
    tm = MOE_TILE

    def row_copy(tile_idx, slot, r):
        tok = src_ref[tile_idx * tm + r]
        return pltpu.make_async_copy(x_hbm.at[pl.ds(tok, 1), :], xbuf.at[slot, pl.ds(r, 1), :],
                                     sem.at[slot])

    def start_tile(tile_idx, slot):
        def body(r, c):
            row_copy(tile_idx, slot, r).start()
            return c
        lax.fori_loop(0, tm, body, 0)

    def wait_tile(tile_idx, slot):
        def body(r, c):
            row_copy(tile_idx, slot, r).wait()
            return c
        lax.fori_loop(0, tm, body, 0)

    @pl.when((i == 0) & (n_used > 0))
    def _():
        start_tile(0, 0)

    @pl.when(i + 1 < n_used)
    def _():
        start_tile(i + 1, (i + 1) % 2)

    @pl.when(i < n_used)
    def _():
        slot = i % 2
        wait_tile(i, slot)
        x = xbuf[slot]
        rowb = rowb_ref[...]
        h = x * (1.0 + sc_ref[0]) + sh_ref[0]
        for b in range(1, n_batch):
            h = jnp.where(rowb == b, x * (1.0 + sc_ref[b]) + sh_ref[b], h)
        hb = h.astype(BF16)
        a = jnp.dot(hb, w1_ref[0], preferred_element_type=F32)
        g = jnp.dot(hb, w3_ref[0], preferred_element_type=F32)
        he = (a * (1.0 / (1.0 + jnp.exp(-a))) * g).astype(BF16)
        o_ref[...] = jnp.dot(he, w2_ref[0], preferred_element_type=F32)

    @pl.when(i >= n_used)
    def _():
        o_ref[...] = jnp.zeros(o_ref.shape, o_ref.dtype)


def _experts(x, scale, shift, tile_expert, src_tok, n_used, row_batch, w1, w3, w2):
    t, d = x.shape
    n_batch = scale.shape[0]
    n_tiles = tile_expert.shape[0]
    de = w1.shape[2]
    tm = MOE_TILE
    grid_spec = pltpu.PrefetchScalarGridSpec(
        num_scalar_prefetch=3,
        grid=(n_tiles,),
        in_specs=[
            pl.BlockSpec(memory_space=pl.ANY),
            pl.BlockSpec((tm, 1), lambda i, te, src, nt: (i, 0)),
            pl.BlockSpec((n_batch, 1, d), lambda i, te, src, nt: (0, 0, 0)),
            pl.BlockSpec((n_batch, 1, d), lambda i, te, src, nt: (0, 0, 0)),
            pl.BlockSpec((1, d, de), lambda i, te, src, nt: (te[i], 0, 0)),
            pl.BlockSpec((1, d, de), lambda i, te, src, nt: (te[i], 0, 0)),
            pl.BlockSpec((1, de, d), lambda i, te, src, nt: (te[i], 0, 0)),
        ],
        out_specs=pl.BlockSpec((tm, d), lambda i, te, src, nt: (i, 0)),
        scratch_shapes=[pltpu.VMEM((2, tm, d), F32), pltpu.SemaphoreType.DMA((2,))],
    )
    return pl.pallas_call(
        functools.partial(_expert_kernel, n_batch=n_batch),
        grid_spec=grid_spec,
        out_shape=jax.ShapeDtypeStruct((n_tiles * tm, d), F32),
        compiler_params=_cparams("arbitrary"),
    )(tile_expert, src_tok, n_used, x, row_batch, scale.reshape(n_batch, 1, d),
      shift.reshape(n_batch, 1, d), w1, w3, w2)


COMBINE_TILE = 256


def _combine_kernel(dst_ref, y_hbm, route_ref, x_ref, gate_ref, g_ref, b_ref, out_ref, ybuf, sem):
    i = pl.program_id(0)
    n_steps = pl.num_programs(0)
    tm = COMBINE_TILE

    def row_copy(step, slot, k, r):
        pos = dst_ref[(step * tm + r) * TOPK_IN_GROUP + k]
        return pltpu.make_async_copy(y_hbm.at[pl.ds(pos, 1), :], ybuf.at[slot, k, pl.ds(r, 1), :],
                                     sem.at[slot])

    def start_step(step, slot):
        def body(r, c):
            for k in range(TOPK_IN_GROUP):
                row_copy(step, slot, k, r).start()
            return c
        lax.fori_loop(0, tm, body, 0)

    def wait_step(step, slot):
        def body(r, c):
            for k in range(TOPK_IN_GROUP):
                row_copy(step, slot, k, r).wait()
            return c
        lax.fori_loop(0, tm, body, 0)

    @pl.when(i == 0)
    def _():
        start_step(0, 0)

    @pl.when(i + 1 < n_steps)
    def _():
        start_step(i + 1, (i + 1) % 2)

    slot = i % 2
    wait_step(i, slot)
    route = route_ref[...]
    lane = lax.broadcasted_iota(jnp.int32, route.shape, 1)
    mix = None
    for k in range(TOPK_IN_GROUP):
        wk = jnp.sum(jnp.where(lane == TOPK_IN_GROUP + k, route, 0.0), axis=-1, keepdims=True)
        term = wk * ybuf[slot, k]
        mix = term if mix is None else mix + term
    y = ALPHA * x_ref[...] + gate_ref[0] * mix
    out_ref[...] = _layer_norm(y, g_ref[...], b_ref[...])


def _combine_ln(dst, y_sorted, route, x, gate, ln_g, ln_b, seq):
    t, d = x.shape
    n_batch = gate.shape[0]
    tm = COMBINE_TILE
    per_batch = seq // tm
    row = lambda w: pl.BlockSpec((tm, w), lambda i, dst: (i, 0))
    vec = pl.BlockSpec((1, d), lambda i, dst: (0, 0))
    grid_spec = pltpu.PrefetchScalarGridSpec(
        num_scalar_prefetch=1,
        grid=(t // tm,),
        in_specs=[pl.BlockSpec(memory_space=pl.ANY), row(LANES), row(d),
                  pl.BlockSpec((1, 1, d), lambda i, dst: (i // per_batch, 0, 0)), vec, vec],
        out_specs=row(d),
        scratch_shapes=[pltpu.VMEM((2, TOPK_IN_GROUP, tm, d), F32), pltpu.SemaphoreType.DMA((2,))],
    )
    return pl.pallas_call(
        _combine_kernel,
        grid_spec=grid_spec,
        out_shape=jax.ShapeDtypeStruct((t, d), F32),
        compiler_params=_cparams("arbitrary"),
    )(dst, y_sorted, route, x, gate.reshape(n_batch, 1, d), ln_g.reshape(1, d), ln_b.reshape(1, d))


def _sort_plan(route, seq):
    t = route.shape[0]
    tm = MOE_TILE
    eid = route[:, :TOPK_IN_GROUP].astype(jnp.int32).reshape(-1)
    onehot = (eid[:, None] == jnp.arange(N_EXPERTS)[None, :]).astype(jnp.int32)
    before = jnp.cumsum(onehot, axis=0) - onehot
    rank = jnp.sum(before * onehot, axis=1)
    counts = jnp.sum(onehot, axis=0)
    tiles = (counts + tm - 1) // tm
    tile_end = jnp.cumsum(tiles)
    tile_start = tile_end - tiles
    dst = tile_start[eid] * tm + rank
    n_tiles = (t * TOPK_IN_GROUP) // tm + N_EXPERTS
    tile_ids = jnp.arange(n_tiles)
    tile_expert = jnp.sum((tile_ids[:, None] >= tile_end[None, :]).astype(jnp.int32), axis=1)
    tile_expert = jnp.minimum(tile_expert, N_EXPERTS - 1)
    n_used = tile_end[-1:].astype(jnp.int32)
    tok = jnp.arange(t * TOPK_IN_GROUP, dtype=jnp.int32) // TOPK_IN_GROUP
    src_tok = jnp.zeros((n_tiles * tm,), jnp.int32).at[dst].set(tok)
    row_batch = (src_tok // seq).reshape(-1, 1)
    return dst.astype(jnp.int32), tile_expert.astype(jnp.int32), src_tok, n_used, row_batch


def _moe_layer(x, scale, shift, gate, w_group, b_group, w_router, b_router, w1, w3, w2,
               ln_g, ln_b, seq):
    route = _router(x, scale, shift, w_group, b_group, w_router, b_router, seq)
    dst, tile_expert, src_tok, n_used, row_batch = _sort_plan(route, seq)
    y_sorted = _experts(x, scale, shift, tile_expert, src_tok, n_used, row_batch,
                        w1.astype(BF16), w3.astype(BF16), w2.astype(BF16))
    return _combine_ln(dst, y_sorted, route, x, gate, ln_g, ln_b, seq)


def _swa_layer(x, scale, shift, gate, tables, w_qkv, b_qkv, sinks, w_o, ln_g, ln_b, seq):
    qw = Q_HEADS * HEAD_DIM
    kvw = KV_HEADS * HEAD_DIM
    tn = 256
    qkv = _mod_proj(x, scale, shift, w_qkv.astype(BF16), b_qkv, tables, seq,
                    rope_lo=qw // tn, rope_hi=(qw + kvw) // tn, tn=tn)
    q = qkv[:, :qw]
    k = qkv[:, qw:qw + kvw]
    v = qkv[:, qw + kvw:]
    o = _band_attention(q, k, v, tables, seq, SWA_WINDOW, sinks=sinks)
    return _out_proj_ln([o], w_o.astype(BF16), x, gate, ln_g, ln_b, seq)


def _nsa_layer(x, scale, shift, gate, tables, w_in, pe_k, pe_v, phi_k1, phi_k2, phi_v1, phi_v2,
               w_o, ln_g, ln_b, seq):
    t, d = x.shape
    n_batch = t // seq
    qw = Q_HEADS * HEAD_DIM
    kvw = KV_HEADS * HEAD_DIM
    tn = 256
    cols = [qw + i * kvw for i in range(7)]
    w_q, w_kc, w_vc, w_ks, w_vs, w_kw, w_vw, w_g = jnp.split(w_in, cols, axis=1)
    w_g = w_g.reshape(d, Q_HEADS, 3).transpose(0, 2, 1).reshape(d, 3 * Q_HEADS)
    w_g = jnp.pad(w_g, ((0, 0), (0, tn - 3 * Q_HEADS)))
    w_all = jnp.concatenate([w_q, w_ks, w_kw, w_vs, w_vw, w_kc, w_vc, w_g], axis=1).astype(BF16)
    n_cols = w_all.shape[1]
    proj = _mod_proj(x, scale, shift, w_all, jnp.zeros((n_cols,), F32), tables, seq,
                     rope_lo=qw // tn, rope_hi=(qw + 2 * kvw) // tn, tn=tn)
    q = proj[:, :qw]
    off = qw
    k_s, k_w, v_s, v_w, k_c, v_c = (proj[:, off + i * kvw: off + (i + 1) * kvw] for i in range(6))
    gates = proj[:, off + 6 * kvw: off + 6 * kvw + LANES]

    kc = _compress(k_c, pe_k, phi_k1, phi_k2, seq)
    vc = _compress(v_c, pe_v, phi_v1, phi_v2, seq)
    o_cmp, selb = _cmp_attention(q, kc, vc, gates, seq, gate_col=0)

    nsel = seq // SEL_BLOCK
    heads = lambda a: a.reshape(n_batch, seq, KV_HEADS, HEAD_DIM).transpose(0, 2, 1, 3)
    onehot = (jnp.arange(seq)[:, None] // SEL_BLOCK == jnp.arange(nsel)[None, :]).astype(BF16)
    k_aug = jnp.concatenate(
        [heads(k_s), jnp.broadcast_to(onehot, (n_batch, KV_HEADS, seq, nsel))], axis=-1)
    o_sel = _sel_attention(q, selb, k_aug, heads(v_s), tables, gates, seq, gate_col=Q_HEADS)
    o_win = _band_attention(q, k_w, v_w, tables, seq, NSA_WINDOW, gates=gates, gate_col=2 * Q_HEADS)
    return _out_proj_ln([o_cmp, o_sel, o_win], w_o.astype(BF16), x, gate, ln_g, ln_b, seq)


def kernel(x, c, positions, w_ada, b_ada, swa_w_qkv, swa_b_qkv, swa_sinks, swa_w_o, nsa_w_in,
           nsa_pe_k, nsa_pe_v, nsa_phi_k1, nsa_phi_k2, nsa_phi_v1, nsa_phi_v2, nsa_w_o,
           moe_w_group, moe_b_group, moe_w_router, moe_b_router, moe_w1, moe_w3, moe_w2,
           ln_t_g, ln_t_b, ln_c_g, ln_c_b):
    n_batch, seq, d = x.shape
    depth = w_ada.shape[0]
    xt = x.reshape(n_batch * seq, d)
    mod = _adaln_mod(c, w_ada, b_ada)
    tables = _rope_tables(positions)
    for i in range(depth):
        sh_t, sc_t, g_t, sh_c, sc_c, g_c = (mod[i, :, k * d:(k + 1) * d] for k in range(6))
        j = i // 2
        if i % 2 == 0:
            xt = _swa_layer(xt, sc_t, sh_t, g_t, tables, swa_w_qkv[j], swa_b_qkv[j], swa_sinks[j],
                            swa_w_o[j], ln_t_g[i], ln_t_b[i], seq)
        else:
            xt = _nsa_layer(xt, sc_t, sh_t, g_t, tables, nsa_w_in[j], nsa_pe_k[j], nsa_pe_v[j],
                            nsa_phi_k1[j], nsa_phi_k2[j], nsa_phi_v1[j], nsa_phi_v2[j], nsa_w_o[j],
                            ln_t_g[i], ln_t_b[i], seq)
        xt = _moe_layer(xt, sc_c, sh_c, g_c, moe_w_group[i], moe_b_group[i], moe_w_router[i],
                        moe_b_router[i], moe_w1[i], moe_w3[i], moe_w2[i], ln_c_g[i], ln_c_b[i], seq)
    return xt.reshape(n_batch, seq, d)
```

```python
import functools

import jax
import jax.numpy as jnp
from jax import lax
from jax.experimental import pallas as pl
from jax.experimental.pallas import tpu as pltpu

F32 = jnp.float32
BF16 = jnp.bfloat16

HEAD_DIM = 64
ROPE_DIM = HEAD_DIM // 4
ROPE_HALF = ROPE_DIM // 2
ROPE_THETA = 500000.0
Q_HEADS = 32
KV_HEADS = 4
GROUP = Q_HEADS // KV_HEADS
SWA_WINDOW = 128
NSA_WINDOW = 512
CMP_BLOCK = 32
CMP_STRIDE = 16
SEL_BLOCK = 64
SEL_TOPK = 16
FORCE_BONUS = 1e4
N_GROUPS = 4
EXPERTS_PER_GROUP = 4
N_EXPERTS = N_GROUPS * EXPERTS_PER_GROUP
TOPK_IN_GROUP = 2
DEPTH = 2
ALPHA = (2 * DEPTH) ** 0.25
LN_EPS = 1e-5
NEG_INF = -1e30
PICKED = -3e38

LANES = 128
ATTN_BLOCK = 128
VMEM_LIMIT = 56 * 1024 * 1024


def _cparams(*sem):
    return pltpu.CompilerParams(dimension_semantics=sem, vmem_limit_bytes=VMEM_LIMIT)


def _mod_kernel(cb_ref, w_ref, b_ref, o_ref, cs_ref, *, n_batch, tn):
    @pl.when((pl.program_id(0) == 0) & (pl.program_id(1) == 0))
    def _():
        c = cb_ref[...]
        cs_ref[...] = c * (1.0 / (1.0 + jnp.exp(-c)))

    for b in range(n_batch):
        cs = cs_ref[b]
        parts = []
        for g in range(tn // LANES):
            wg = w_ref[0, :, g * LANES:(g + 1) * LANES]
            parts.append(jnp.sum(wg * cs, axis=0, keepdims=True))
        o_ref[0, b:b + 1, :] = jnp.concatenate(parts, axis=-1) + b_ref[0]


def _adaln_mod(c, w_ada, b_ada):
    n_batch, d = c.shape
    depth, _, n = w_ada.shape
    tn = 512
    cb = jnp.broadcast_to(c[:, :, None], (n_batch, d, LANES))
    return pl.pallas_call(
        functools.partial(_mod_kernel, n_batch=n_batch, tn=tn),
        grid=(depth, n // tn),
        in_specs=[
            pl.BlockSpec((n_batch, d, LANES), lambda l, j: (0, 0, 0)),
            pl.BlockSpec((1, d, tn), lambda l, j: (l, 0, j)),
            pl.BlockSpec((1, 1, tn), lambda l, j: (l, 0, j)),
        ],
        out_specs=pl.BlockSpec((1, n_batch, tn), lambda l, j: (l, 0, j)),
        out_shape=jax.ShapeDtypeStruct((depth, n_batch, n), F32),
        scratch_shapes=[pltpu.VMEM((n_batch, d, LANES), F32)],
        compiler_params=_cparams("arbitrary", "arbitrary"),
    )(cb, w_ada, b_ada.reshape(depth, 1, n))


def _rope_table_kernel(pos_ref, inv_ref, c_ref, s1_ref, s2_ref):
    ang = pos_ref[...] * inv_ref[...]
    d = lax.broadcasted_iota(jnp.int32, ang.shape, 1) % HEAD_DIM
    cos = jnp.cos(ang)
    sin = jnp.sin(ang)
    c_ref[...] = jnp.where(d < ROPE_DIM, cos, 1.0)
    s1_ref[...] = jnp.where(d < ROPE_HALF, -sin, 0.0)
    s2_ref[...] = jnp.where((d >= ROPE_HALF) & (d < ROPE_DIM), sin, 0.0)


def _rope_tables(positions):
    t = positions.size
    tm = 1024
    inv_freq = ROPE_THETA ** (-jnp.arange(0, ROPE_DIM, 2, dtype=F32) / ROPE_DIM)
    lane = jnp.arange(LANES) % HEAD_DIM
    inv_lane = inv_freq[lane % ROPE_HALF].reshape(1, LANES)
    pos = positions.astype(F32).reshape(t, 1)
    spec = pl.BlockSpec((tm, LANES), lambda i: (i, 0))
    return pl.pallas_call(
        _rope_table_kernel,
        grid=(t // tm,),
        in_specs=[pl.BlockSpec((tm, 1), lambda i: (i, 0)),
                  pl.BlockSpec((1, LANES), lambda i: (0, 0))],
        out_specs=[spec, spec, spec],
        out_shape=[jax.ShapeDtypeStruct((t, LANES), F32)] * 3,
        compiler_params=_cparams("arbitrary"),
    )(pos, inv_lane)


def _apply_rope(x, c, s1, s2):
    reps = x.shape[-1] // LANES
    if reps > 1:
        c = jnp.concatenate([c] * reps, axis=-1)
        s1 = jnp.concatenate([s1] * reps, axis=-1)
        s2 = jnp.concatenate([s2] * reps, axis=-1)
    n = x.shape[-1]
    up = pltpu.roll(x, n - ROPE_HALF, 1)
    down = pltpu.roll(x, ROPE_HALF, 1)
    return x * c + up * s1 + down * s2


def _proj_kernel(x_ref, sc_ref, sh_ref, w_ref, b_ref, c_ref, s1_ref, s2_ref, o_ref, h_ref,
                 *, rope_lo, rope_hi):
    j = pl.program_id(1)

    @pl.when(j == 0)
    def _():
        h_ref[...] = (x_ref[...] * (1.0 + sc_ref[0]) + sh_ref[0]).astype(BF16)

    acc = jnp.dot(h_ref[...], w_ref[...], preferred_element_type=F32) + b_ref[...]
    if rope_hi > rope_lo:
        is_rope = (j >= rope_lo) & (j < rope_hi)

        @pl.when(is_rope)
        def _():
            o_ref[...] = _apply_rope(acc, c_ref[...], s1_ref[...], s2_ref[...]).astype(o_ref.dtype)

        @pl.when(jnp.logical_not(is_rope))
        def _():
            o_ref[...] = acc.astype(o_ref.dtype)
    else:
        o_ref[...] = acc.astype(o_ref.dtype)


def _mod_proj(x, scale, shift, w, bias, tables, seq, *, rope_lo=0, rope_hi=0, tn=256, tm=1024,
              out_dtype=BF16):
    t, d = x.shape
    n = w.shape[1]
    n_batch = scale.shape[0]
    tm = min(tm, seq)
    per_batch = seq // tm
    c_tab, s1_tab, s2_tab = tables
    vec = pl.BlockSpec((1, 1, d), lambda i, j: (i // per_batch, 0, 0))
    tab = pl.BlockSpec((tm, LANES), lambda i, j: (i, 0))
    return pl.pallas_call(
        functools.partial(_proj_kernel, rope_lo=rope_lo, rope_hi=rope_hi),
        grid=(t // tm, n // tn),
        in_specs=[
            pl.BlockSpec((tm, d), lambda i, j: (i, 0)),
            vec, vec,
            pl.BlockSpec((d, tn), lambda i, j: (0, j)),
            pl.BlockSpec((1, tn), lambda i, j: (0, j)),
            tab, tab, tab,
        ],
        out_specs=pl.BlockSpec((tm, tn), lambda i, j: (i, j)),
        out_shape=jax.ShapeDtypeStruct((t, n), out_dtype),
        scratch_shapes=[pltpu.VMEM((tm, d), BF16)],
        compiler_params=_cparams("arbitrary", "arbitrary"),
    )(x, scale.reshape(n_batch, 1, d), shift.reshape(n_batch, 1, d), w, bias.reshape(1, n),
      c_tab, s1_tab, s2_tab)


def _scaled_q(q_ref, tables=None):
    q = q_ref[...].astype(F32)
    if tables is not None:
        q = _apply_rope(q, *(t[...] for t in tables))
    return q * (HEAD_DIM ** -0.5)


def _split_even_odd(q):
    even = (lax.broadcasted_iota(jnp.int32, q.shape, 1) % LANES) < HEAD_DIM
    return jnp.where(even, q, 0.0).astype(BF16), jnp.where(even, 0.0, q).astype(BF16)


def _stack_heads(q_even, q_odd, first_head):
    parts = []
    for g in range(GROUP):
        head = first_head + g
        src = q_even if head % 2 == 0 else q_odd
        parts.append(src[:, (head // 2) * LANES:(head // 2 + 1) * LANES])
    return jnp.concatenate(parts, axis=0)


def _twice(x):
    return jnp.concatenate([x, x], axis=-1)


def _merge_pairs(o, rows, scale):
    even = lax.broadcasted_iota(jnp.int32, (rows, LANES), 1) < HEAD_DIM
    out = []
    for g in range(0, GROUP, 2):
        a = o[g * rows:(g + 1) * rows] * scale[g]
        b = o[(g + 1) * rows:(g + 2) * rows] * scale[g + 1]
        out.append(jnp.where(even, a, b))
    return jnp.concatenate(out, axis=-1)


def _sigmoid(x):
    return 1.0 / (1.0 + jnp.exp(-x))


def _band_attn_kernel(*refs, window, use_sinks, use_gate, gate_col):
    it = iter(refs)
    q_ref, k_ref, v_ref, c_ref, s1_ref, s2_ref = (next(it) for _ in range(6))
    sink_ref = next(it) if use_sinks else None
    gate_ref = next(it) if use_gate else None
    o_ref = next(it)

    n = pl.program_id(1)
    rows = ATTN_BLOCK
    halo = -(-window // rows) * rows
    span = rows + halo
    start = jnp.maximum(n * rows - halo, 0)
    start = pl.multiple_of(start, rows)
    q_even, q_odd = _split_even_odd(_scaled_q(q_ref, (c_ref, s1_ref, s2_ref)))
    k = k_ref[pl.ds(start, span), :]
    v = v_ref[pl.ds(start, span), :]
    qpos = n * rows + lax.broadcasted_iota(jnp.int32, (rows, span), 0)
    kpos = start + lax.broadcasted_iota(jnp.int32, (rows, span), 1)
    rel = qpos - kpos
    mask = (rel >= 0) & (rel < window)
    if use_gate:
        gate = _sigmoid(gate_ref[...].astype(F32))

    for j in range(KV_HEADS):
        k2 = _twice(k[:, j * HEAD_DIM:(j + 1) * HEAD_DIM])
        v2 = _twice(v[:, j * HEAD_DIM:(j + 1) * HEAD_DIM])
        q8 = _stack_heads(q_even, q_odd, j * GROUP)
        s = lax.dot_general(q8, k2, (((1,), (1,)), ((), ())), preferred_element_type=F32)
        ps, scale = [], []
        for g in range(GROUP):
            sg = jnp.where(mask, s[g * rows:(g + 1) * rows], NEG_INF)
            m = jnp.max(sg, axis=-1, keepdims=True)
            if use_sinks:
                sink = sink_ref[j * GROUP + g]
                m = jnp.maximum(m, sink)
            e = jnp.exp(sg - m)
            den = jnp.sum(e, axis=-1, keepdims=True)
            if use_sinks:
                den = den + jnp.exp(sink - m)
            inv = 1.0 / den
            if use_gate:
                col = gate_col + j * GROUP + g
                inv = inv * gate[:, col:col + 1]
            ps.append(e.astype(BF16))
            scale.append(inv)
        o = jnp.dot(jnp.concatenate(ps, axis=0), v2, preferred_element_type=F32)
        o_ref[:, j * GROUP * HEAD_DIM:(j + 1) * GROUP * HEAD_DIM] = (
            _merge_pairs(o, rows, scale).astype(o_ref.dtype))


def _band_attention(proj, q_blk, k_blk, v_blk, tables, seq, window, sinks=None, gate_blk=None,
                    gate_col=0):
    t = proj.shape[0]
    n_batch = t // seq
    nb = seq // ATTN_BLOCK
    qw = Q_HEADS * HEAD_DIM
    kvw = KV_HEADS * HEAD_DIM
    row = lambda w, c: pl.BlockSpec((ATTN_BLOCK, w), lambda b, i: (b * nb + i, c))
    res = lambda c: pl.BlockSpec((seq, kvw), lambda b, i: (b, c))
    in_specs = [row(qw, q_blk), res(k_blk), res(v_blk), row(LANES, 0), row(LANES, 0), row(LANES, 0)]
    args = [proj, proj, proj, *tables]
    if sinks is not None:
        in_specs.append(pl.BlockSpec(memory_space=pltpu.SMEM))
        args.append(sinks)
    if gate_blk is not None:
        in_specs.append(row(LANES, gate_blk))
        args.append(proj)
    return pl.pallas_call(
        functools.partial(_band_attn_kernel, window=window, use_sinks=sinks is not None,
                          use_gate=gate_blk is not None, gate_col=gate_col),
        grid=(n_batch, nb),
        in_specs=in_specs,
        out_specs=row(qw, 0),
        out_shape=jax.ShapeDtypeStruct((t, qw), BF16),
        compiler_params=_cparams("arbitrary", "arbitrary"),
    )(*args)


def _layer_norm(y, g, b):
    mu = jnp.mean(y, axis=-1, keepdims=True)
    yc = y - mu
    var = jnp.mean(yc * yc, axis=-1, keepdims=True)
    return yc * lax.rsqrt(var + LN_EPS) * g + b


def _out_proj_kernel(*refs, n_parts):
    o_parts = refs[:n_parts]
    w_ref, x_ref, gate_ref, g_ref, b_ref, out_ref = refs[n_parts:]
    if n_parts == 1:
        o = o_parts[0][...]
    else:
        acc = o_parts[0][...].astype(F32)
        for r in o_parts[1:]:
            acc = acc + r[...].astype(F32)
        o = acc.astype(BF16)
    mix = jnp.dot(o, w_ref[...], preferred_element_type=F32)
    y = ALPHA * x_ref[...] + gate_ref[0] * mix
    out_ref[...] = _layer_norm(y, g_ref[...], b_ref[...])


def _out_proj_ln(o_parts, w_o, x, gate, ln_g, ln_b, seq, tm=256):
    t, d = x.shape
    n_batch = gate.shape[0]
    per_batch = seq // tm
    k = w_o.shape[0]
    row_o = pl.BlockSpec((tm, k), lambda i: (i, 0))
    row_x = pl.BlockSpec((tm, d), lambda i: (i, 0))
    vec = pl.BlockSpec((1, d), lambda i: (0, 0))
    return pl.pallas_call(
        functools.partial(_out_proj_kernel, n_parts=len(o_parts)),
        grid=(t // tm,),
        in_specs=[row_o] * len(o_parts) + [
            pl.BlockSpec((k, d), lambda i: (0, 0)),
            row_x,
            pl.BlockSpec((1, 1, d), lambda i: (i // per_batch, 0, 0)),
            vec, vec],
        out_specs=row_x,
        out_shape=jax.ShapeDtypeStruct((t, d), F32),
        compiler_params=_cparams("arbitrary"),
    )(*o_parts, w_o, x, gate.reshape(n_batch, 1, d), ln_g.reshape(1, d), ln_b.reshape(1, d))


def _compress_kernel(x_ref, pe_ref, w1_ref, w1a_ref, w1b_ref, w2_ref, o_ref):
    x = x_ref[0]
    a = jnp.dot(x, w1a_ref[...], preferred_element_type=F32)
    b = jnp.dot(x, w1b_ref[...], preferred_element_type=F32)
    nc = a.shape[0]
    b_next = pltpu.roll(b, nc - 1, 0)
    pe_term = jnp.dot(pe_ref[...], w1_ref[...], preferred_element_type=F32)[0:1]
    hid = a + b_next + jnp.concatenate([pe_term] * KV_HEADS, axis=-1)
    hid = hid * (1.0 / (1.0 + jnp.exp(-hid)))
    o_ref[0] = jnp.dot(hid.astype(BF16), w2_ref[...], preferred_element_type=F32).astype(o_ref.dtype)


def _block_diag_heads(w):
    p, d, n = w.shape
    eye = jnp.eye(KV_HEADS, dtype=w.dtype)
    big = w[:, None, :, None, :] * eye[None, :, None, :, None]
    return big.reshape(p * KV_HEADS * d, KV_HEADS * n)


def _compress(xc, pe, w1, w2, seq):
    t, kvw = xc.shape
    n_batch = t // seq
    nch = seq // CMP_STRIDE
    hidden = w1.shape[1]
    x = xc.reshape(n_batch, nch, CMP_STRIDE * kvw)
    w1r = w1.reshape(CMP_BLOCK, HEAD_DIM, hidden)
    w1a = _block_diag_heads(w1r[:CMP_STRIDE]).astype(BF16)
    w1b = _block_diag_heads(w1r[CMP_STRIDE:]).astype(BF16)
    eye = jnp.eye(KV_HEADS, dtype=w2.dtype)
    w2d = (w2[None, :, None, :] * eye[:, None, :, None]).reshape(KV_HEADS * hidden, kvw).astype(BF16)
    pe8 = jnp.broadcast_to(pe.reshape(1, CMP_BLOCK * HEAD_DIM), (8, CMP_BLOCK * HEAD_DIM)).astype(BF16)
    full = lambda a: pl.BlockSpec(a.shape, lambda b: (0,) * a.ndim)
    w1b16 = w1.astype(BF16)
    return pl.pallas_call(
        _compress_kernel,
        grid=(n_batch,),
        in_specs=[pl.BlockSpec((1, nch, CMP_STRIDE * kvw), lambda b: (b, 0, 0)),
                  full(pe8), full(w1b16), full(w1a), full(w1b), full(w2d)],
        out_specs=pl.BlockSpec((1, nch, kvw), lambda b: (b, 0, 0)),
        out_shape=jax.ShapeDtypeStruct((n_batch, nch, kvw), BF16),
        compiler_params=_cparams("arbitrary"),
    )(x, pe8, w1b16, w1a, w1b, w2d)


def _topk_bias(score, top_k):
    nblk, rows = score.shape
    sub = lax.broadcasted_iota(jnp.int32, (8, rows), 0)
    chunks = [score[c * 8:(c + 1) * 8] for c in range(nblk // 8)]
    counts = [jnp.zeros((8, rows), F32) for _ in chunks]
    for sp in range(nblk):
        row = score[sp:sp + 1]
        for c, chunk in enumerate(chunks):
            ge = jnp.where(row >= chunk, 1.0, 0.0)
            gt = jnp.where(row > chunk, 1.0, 0.0)
            if sp < c * 8:
                beats = ge
            elif sp >= (c + 1) * 8:
                beats = gt
            else:
                beats = jnp.where(sub > sp - c * 8, ge, gt)
            counts[c] = counts[c] + beats
    return jnp.concatenate([jnp.where(cnt < top_k, 0.0, NEG_INF) for cnt in counts], axis=0)


def _cmp_attn_kernel(q_ref, kc_ref, vc_ref, ovt_ref, gate_ref, o_ref, sel_ref, *, gate_col, top_k):
    n = pl.program_id(1)
    rows = ATTN_BLOCK
    nc = kc_ref.shape[1]
    nblk = ovt_ref.shape[0]
    q_even, q_odd = _split_even_odd(_scaled_q(q_ref))
    kc = kc_ref[0]
    vc = vc_ref[0]
    t = n * rows + lax.broadcasted_iota(jnp.int32, (rows, nc), 0)
    cmp_end = lax.broadcasted_iota(jnp.int32, (rows, nc), 1) * CMP_STRIDE + (CMP_BLOCK - 1)
    valid = cmp_end <= t
    has_valid = n * rows + lax.broadcasted_iota(jnp.int32, (rows, 1), 0) >= CMP_BLOCK - 1
    gate = _sigmoid(gate_ref[...].astype(F32))

    ts = n * rows + lax.broadcasted_iota(jnp.int32, (nblk, rows), 1)
    blk = lax.broadcasted_iota(jnp.int32, (nblk, rows), 0)
    cur = lax.shift_right_arithmetic(ts, SEL_BLOCK.bit_length() - 1)
    causal = blk * SEL_BLOCK <= ts
    forced = (blk == 0) | (blk == cur) | (blk == cur - 1)
    bonus = jnp.where(forced, FORCE_BONUS, 0.0)

    for j in range(KV_HEADS):
        k2 = _twice(kc[:, j * HEAD_DIM:(j + 1) * HEAD_DIM])
        v2 = _twice(vc[:, j * HEAD_DIM:(j + 1) * HEAD_DIM])
        q8 = _stack_heads(q_even, q_odd, j * GROUP)
        s = lax.dot_general(q8, k2, (((1,), (1,)), ((), ())), preferred_element_type=F32)
        ps, scale, psum = [], [], None
        for g in range(GROUP):
            sg = jnp.where(valid, s[g * rows:(g + 1) * rows], NEG_INF)
            m = jnp.max(sg, axis=-1, keepdims=True)
            e = jnp.exp(sg - m)
            den = jnp.sum(e, axis=-1, keepdims=True)
            inv = jnp.where(has_valid, 1.0 / den, 0.0)
            pn = e * inv
            psum = pn if psum is None else psum + pn
            col = gate_col + j * GROUP + g
            ps.append(e.astype(BF16))
            scale.append(inv * gate[:, col:col + 1])
        o = jnp.dot(jnp.concatenate(ps, axis=0), v2, preferred_element_type=F32)
        o_ref[:, j * GROUP * HEAD_DIM:(j + 1) * GROUP * HEAD_DIM] = (
            _merge_pairs(o, rows, scale).astype(o_ref.dtype))

        imp_t = lax.dot_general(ovt_ref[...], psum, (((1,), (1,)), ((), ())),
                                preferred_element_type=F32,
                                precision=lax.Precision.HIGHEST)
        score = jnp.where(causal, imp_t + bonus, NEG_INF)
        bias_t = _topk_bias(score, top_k)
        sel_ref[0, j] = jnp.concatenate([bias_t, bias_t], axis=0).T.astype(sel_ref.dtype)


def _cmp_attention(proj, kc, vc, gate_blk, seq, gate_col):
    t = proj.shape[0]
    n_batch = t // seq
    nb = seq // ATTN_BLOCK
    nc = kc.shape[1]
    nsel = seq // SEL_BLOCK
    assert nsel <= HEAD_DIM
    qw = Q_HEADS * HEAD_DIM
    top_k = min(SEL_TOPK, nsel)
    c0 = jnp.arange(nc)[None, :] * CMP_STRIDE
    s0 = jnp.arange(HEAD_DIM)[:, None] * SEL_BLOCK
    ov = jnp.clip(jnp.minimum(c0 + CMP_BLOCK, s0 + SEL_BLOCK) - jnp.maximum(c0, s0), 0)
    overlap_t = ov.astype(F32) / CMP_BLOCK
    row = lambda w, c: pl.BlockSpec((ATTN_BLOCK, w), lambda b, i: (b * nb + i, c))
    res = pl.BlockSpec((1, nc, kc.shape[2]), lambda b, i: (b, 0, 0))
    return pl.pallas_call(
        functools.partial(_cmp_attn_kernel, gate_col=gate_col, top_k=top_k),
        grid=(n_batch, nb),
        in_specs=[row(qw, 0), res, res,
                  pl.BlockSpec((HEAD_DIM, nc), lambda b, i: (0, 0)), row(LANES, gate_blk)],
        out_specs=[row(qw, 0),
                   pl.BlockSpec((1, KV_HEADS, ATTN_BLOCK, LANES), lambda b, i: (b, 0, i, 0))],
        out_shape=[jax.ShapeDtypeStruct((t, qw), BF16),
                   jax.ShapeDtypeStruct((n_batch, KV_HEADS, seq, LANES), BF16)],
        compiler_params=_cparams("arbitrary", "arbitrary"),
    )(proj, kc, vc, overlap_t, proj)


SEL_TILE = 512


def _sel_attn_kernel(q_ref, sel_ref, kae_ref, kao_ref, v_ref, c_ref, s1_ref, s2_ref, gate_ref, o_ref,
                     m_ref, l_ref, acc_ref, *, gate_col):
    j = pl.program_id(1)
    n = pl.program_id(2)
    rows = ATTN_BLOCK
    tk = SEL_TILE
    pairs = GROUP // 2
    q = _scaled_q(q_ref, (c_ref, s1_ref, s2_ref))
    bias2 = sel_ref[0, 0].astype(F32)
    low = lax.broadcasted_iota(jnp.int32, (rows, LANES), 1) < HEAD_DIM
    lhs = []
    for parity in range(2):
        parts = []
        for i in range(pairs):
            slab = q[:, i * LANES:(i + 1) * LANES]
            parts.append(jnp.where(low, slab, bias2) if parity == 0 else jnp.where(low, bias2, slab))
        lhs.append(jnp.concatenate(parts, axis=0).astype(BF16))
    ka_refs = (kae_ref, kao_ref)

    m_ref[...] = jnp.full(m_ref.shape, NEG_INF, F32)
    l_ref[...] = jnp.zeros(l_ref.shape, F32)
    acc_ref[...] = jnp.zeros(acc_ref.shape, F32)

    def tile(kt, mask_bias):
        off = pl.multiple_of(kt * tk, tk)
        v2 = v_ref[0, 0, pl.ds(off, tk), :]
        for parity in range(2):
            ka = ka_refs[parity][0, 0, pl.ds(off, tk), :]
            s = lax.dot_general(lhs[parity], ka, (((1,), (1,)), ((), ())),
                                preferred_element_type=F32)
            if mask_bias is not None:
                s = s + mask_bias
            chunks = [s[:, c * LANES:(c + 1) * LANES] for c in range(tk // LANES)]
            rmax = chunks[0]
            for ch in chunks[1:]:
                rmax = jnp.maximum(rmax, ch)
            m_old = m_ref[parity]
            m_new = jnp.maximum(m_old, jnp.max(rmax, axis=-1, keepdims=True))
            alpha = jnp.exp(m_old - m_new)
            ps = [jnp.exp(ch - m_new) for ch in chunks]
            lsum = ps[0]
            for x in ps[1:]:
                lsum = lsum + x
            l_ref[parity] = l_ref[parity] * alpha + lsum
            p = jnp.concatenate([x.astype(BF16) for x in ps], axis=-1)
            acc_ref[parity] = acc_ref[parity] * alpha + jnp.dot(p, v2, preferred_element_type=F32)
            m_ref[parity] = m_new

    def body(kt, carry):
        tile(kt, None)
        return carry

    n_full = (n * rows) // tk
    lax.fori_loop(0, n_full, body, 0)
    qpos = n * rows + lax.broadcasted_iota(jnp.int32, (rows, tk), 0)
    kpos = n_full * tk + lax.broadcasted_iota(jnp.int32, (rows, tk), 1)
    causal = jnp.where(kpos <= qpos, 0.0, NEG_INF)
    tile(n_full, jnp.concatenate([causal] * pairs, axis=0))

    gate = _sigmoid(gate_ref[...].astype(F32))
    lane = lax.broadcasted_iota(jnp.int32, gate.shape, 1)
    outs = []
    for i in range(pairs):
        scaled = []
        for parity in range(2):
            col = gate_col + j * GROUP + 2 * i + parity
            gcol = jnp.sum(jnp.where(lane == col, gate, 0.0), axis=-1, keepdims=True)
            l = jnp.sum(l_ref[parity, i * rows:(i + 1) * rows], axis=-1, keepdims=True)
            scaled.append(acc_ref[parity, i * rows:(i + 1) * rows] * (gcol / l))
        outs.append(jnp.where(low, scaled[0], scaled[1]))
    o_ref[...] = jnp.concatenate(outs, axis=-1).astype(o_ref.dtype)


def _sel_attention(proj, selb, ka_even, ka_odd, v2, tables, gate_blk, seq, gate_col):
    t = proj.shape[0]
    n_batch = t // seq
    nb = seq // ATTN_BLOCK
    gw = GROUP * HEAD_DIM
    pairs = GROUP // 2
    row = lambda c: pl.BlockSpec((ATTN_BLOCK, LANES), lambda b, j, i: (b * nb + i, c))
    res = pl.BlockSpec((1, 1, seq, LANES), lambda b, j, i: (b, j, 0, 0))
    acc = pltpu.VMEM((2, pairs * ATTN_BLOCK, LANES), F32)
    return pl.pallas_call(
        functools.partial(_sel_attn_kernel, gate_col=gate_col),
        grid=(n_batch, KV_HEADS, nb),
        in_specs=[
            pl.BlockSpec((ATTN_BLOCK, gw), lambda b, j, i: (b * nb + i, j)),
            pl.BlockSpec((1, 1, ATTN_BLOCK, LANES), lambda b, j, i: (b, j, i, 0)),
            res, res, res, row(0), row(0), row(0), row(gate_blk)],
        out_specs=pl.BlockSpec((ATTN_BLOCK, gw), lambda b, j, i: (b * nb + i, j)),
        out_shape=jax.ShapeDtypeStruct((t, Q_HEADS * HEAD_DIM), BF16),
        scratch_shapes=[acc, acc, acc],
        compiler_params=_cparams("arbitrary", "arbitrary", "arbitrary"),
    )(proj, selb, ka_even, ka_odd, v2, *tables, proj)


def _router_kernel(x_ref, sc_ref, sh_ref, w_ref, b_ref, o_ref):
    h = x_ref[...] * (1.0 + sc_ref[0]) + sh_ref[0]
    logits = jnp.dot(h, w_ref[...], preferred_element_type=F32,
                     precision=lax.Precision.HIGHEST) + b_ref[...]
    lane = lax.broadcasted_iota(jnp.int32, logits.shape, 1)
    big = 1 << 20

    gmask = lane < N_GROUPS
    lg = jnp.where(gmask, logits, -jnp.inf)
    eg = jnp.exp(lg - jnp.max(lg, axis=-1, keepdims=True))
    pg = eg / jnp.sum(eg, axis=-1, keepdims=True)
    g_prob = jnp.max(pg, axis=-1, keepdims=True)
    g_idx = jnp.min(jnp.where((pg == g_prob) & gmask, lane, big), axis=-1, keepdims=True)

    lo = N_GROUPS + g_idx * EXPERTS_PER_GROUP
    emask = (lane >= lo) & (lane < lo + EXPERTS_PER_GROUP)
    le = jnp.where(emask, logits, -jnp.inf)
    ee = jnp.exp(le - jnp.max(le, axis=-1, keepdims=True))
    pe = jnp.where(emask, ee / jnp.sum(ee, axis=-1, keepdims=True), -1.0)
    p1 = jnp.max(pe, axis=-1, keepdims=True)
    i1 = jnp.min(jnp.where(pe == p1, lane, big), axis=-1, keepdims=True)
    pe2 = jnp.where(lane == i1, -1.0, pe)
    p2 = jnp.max(pe2, axis=-1, keepdims=True)
    i2 = jnp.min(jnp.where(pe2 == p2, lane, big), axis=-1, keepdims=True)
    tot = p1 + p2
    w1 = g_prob * (p1 / tot)
    w2 = g_prob * (p2 / tot)
    e1 = (i1 - N_GROUPS).astype(F32)
    e2 = (i2 - N_GROUPS).astype(F32)
    o_ref[...] = jnp.where(lane == 0, e1, jnp.where(lane == 1, e2,
                           jnp.where(lane == 2, w1, jnp.where(lane == 3, w2, 0.0))))


def _router(x, scale, shift, w_group, b_group, w_router, b_router, seq, tm=512):
    t, d = x.shape
    n_batch = scale.shape[0]
    per_batch = seq // tm
    w = jnp.zeros((d, LANES), F32).at[:, :N_GROUPS].set(w_group)
    w = w.at[:, N_GROUPS:N_GROUPS + N_EXPERTS].set(w_router)
    b = jnp.zeros((1, LANES), F32).at[0, :N_GROUPS].set(b_group)
    b = b.at[0, N_GROUPS:N_GROUPS + N_EXPERTS].set(b_router)
    vec = pl.BlockSpec((1, 1, d), lambda i: (i // per_batch, 0, 0))
    return pl.pallas_call(
        _router_kernel,
        grid=(t // tm,),
        in_specs=[pl.BlockSpec((tm, d), lambda i: (i, 0)), vec, vec,
                  pl.BlockSpec((d, LANES), lambda i: (0, 0)),
                  pl.BlockSpec((1, LANES), lambda i: (0, 0))],
        out_specs=pl.BlockSpec((tm, LANES), lambda i: (i, 0)),
        out_shape=jax.ShapeDtypeStruct((t, LANES), F32),
        compiler_params=_cparams("arbitrary"),
    )(x, scale.reshape(n_batch, 1, d), shift.reshape(n_batch, 1, d), w, b)


MOE_TILE = 256


def _expert_kernel(te_ref, src_ref, nt_ref, x_hbm, rowb_ref, sc_ref, sh_ref, w1_ref, w3_ref, w2_ref,
                   o_ref, xbuf, sem, *, n_batch):
    i = pl.program_id(0)
    n_used = nt_ref[0]
    tm = MOE_TILE

    def row_copy(tile_idx, slot, r):
        tok = src_ref[tile_idx * tm + r]
        return pltpu.make_async_copy(x_hbm.at[pl.ds(tok, 1), :], xbuf.at[slot, pl.ds(r, 1), :],
                                     sem.at[slot])

    def start_tile(tile_idx, slot):
        def body(r, c):
            row_copy(tile_idx, slot, r).start()
            return c
        lax.fori_loop(0, tm, body, 0)

    def wait_tile(tile_idx, slot):
        def body(r, c):
            row_copy(tile_idx, slot, r).wait()
            return c
        lax.fori_loop(0, tm, body, 0)

    @pl.when((i == 0) & (n_used > 0))
    def _():
        start_tile(0, 0)

    @pl.when(i + 1 < n_used)
    def _():
        start_tile(i + 1, (i + 1) % 2)

    @pl.when(i < n_used)
    def _():
        slot = i % 2
        wait_tile(i, slot)
        x = xbuf[slot]
        rowb = rowb_ref[...]
        h = x * (1.0 + sc_ref[0]) + sh_ref[0]
        for b in range(1, n_batch):
            h = jnp.where(rowb == b, x * (1.0 + sc_ref[b]) + sh_ref[b], h)
        hb = h.astype(BF16)
        a = jnp.dot(hb, w1_ref[0], preferred_element_type=F32)
        g = jnp.dot(hb, w3_ref[0], preferred_element_type=F32)
        he = (a * (1.0 / (1.0 + jnp.exp(-a))) * g).astype(BF16)
        o_ref[...] = jnp.dot(he, w2_ref[0], preferred_element_type=F32)

    @pl.when(i >= n_used)
    def _():
        o_ref[...] = jnp.zeros(o_ref.shape, o_ref.dtype)


def _experts(x, scale, shift, tile_expert, src_tok, n_used, row_batch, w1, w3, w2):
    t, d = x.shape
    n_batch = scale.shape[0]
    n_tiles = tile_expert.shape[0]
    de = w1.shape[2]
    tm = MOE_TILE
    grid_spec = pltpu.PrefetchScalarGridSpec(
        num_scalar_prefetch=3,
        grid=(n_tiles,),
        in_specs=[
            pl.BlockSpec(memory_space=pl.ANY),
            pl.BlockSpec((tm, 1), lambda i, te, src, nt: (i, 0)),
            pl.BlockSpec((n_batch, 1, d), lambda i, te, src, nt: (0, 0, 0)),
            pl.BlockSpec((n_batch, 1, d), lambda i, te, src, nt: (0, 0, 0)),
            pl.BlockSpec((1, d, de), lambda i, te, src, nt: (te[i], 0, 0)),
            pl.BlockSpec((1, d, de), lambda i, te, src, nt: (te[i], 0, 0)),
            pl.BlockSpec((1, de, d), lambda i, te, src, nt: (te[i], 0, 0)),
        ],
        out_specs=pl.BlockSpec((tm, d), lambda i, te, src, nt: (i, 0)),
        scratch_shapes=[pltpu.VMEM((2, tm, d), F32), pltpu.SemaphoreType.DMA((2,))],
    )
    return pl.pallas_call(
        functools.partial(_expert_kernel, n_batch=n_batch),
        grid_spec=grid_spec,
        out_shape=jax.ShapeDtypeStruct((n_tiles * tm, d), F32),
        compiler_params=_cparams("arbitrary"),
    )(tile_expert, src_tok, n_used, x, row_batch, scale.reshape(n_batch, 1, d),
      shift.reshape(n_batch, 1, d), w1, w3, w2)


COMBINE_TILE = 256


def _combine_kernel(dst_ref, y_hbm, route_ref, x_ref, gate_ref, g_ref, b_ref, out_ref, ybuf, sem):
    i = pl.program_id(0)
    n_steps = pl.num_programs(0)
    tm = COMBINE_TILE

    def row_copy(step, slot, k, r):
        pos = dst_ref[(step * tm + r) * TOPK_IN_GROUP + k]
        return pltpu.make_async_copy(y_hbm.at[pl.ds(pos, 1), :], ybuf.at[slot, k, pl.ds(r, 1), :],
                                     sem.at[slot])

    def start_step(step, slot):
        def body(r, c):
            for k in range(TOPK_IN_GROUP):
                row_copy(step, slot, k, r).start()
            return c
        lax.fori_loop(0, tm, body, 0)

    def wait_step(step, slot):
        def body(r, c):
            for k in range(TOPK_IN_GROUP):
                row_copy(step, slot, k, r).wait()
            return c
        lax.fori_loop(0, tm, body, 0)

    @pl.when(i == 0)
    def _():
        start_step(0, 0)

    @pl.when(i + 1 < n_steps)
    def _():
        start_step(i + 1, (i + 1) % 2)

    slot = i % 2
    wait_step(i, slot)
    route = route_ref[...]
    lane = lax.broadcasted_iota(jnp.int32, route.shape, 1)
    mix = None
    for k in range(TOPK_IN_GROUP):
        wk = jnp.sum(jnp.where(lane == TOPK_IN_GROUP + k, route, 0.0), axis=-1, keepdims=True)
        term = wk * ybuf[slot, k]
        mix = term if mix is None else mix + term
    y = ALPHA * x_ref[...] + gate_ref[0] * mix
    out_ref[...] = _layer_norm(y, g_ref[...], b_ref[...])


def _combine_ln(dst, y_sorted, route, x, gate, ln_g, ln_b, seq):
    t, d = x.shape
    n_batch = gate.shape[0]
    tm = COMBINE_TILE
    per_batch = seq // tm
    row = lambda w: pl.BlockSpec((tm, w), lambda i, dst: (i, 0))
    vec = pl.BlockSpec((1, d), lambda i, dst: (0, 0))
    grid_spec = pltpu.PrefetchScalarGridSpec(
        num_scalar_prefetch=1,
        grid=(t // tm,),
        in_specs=[pl.BlockSpec(memory_space=pl.ANY), row(LANES), row(d),
                  pl.BlockSpec((1, 1, d), lambda i, dst: (i // per_batch, 0, 0)), vec, vec],
        out_specs=row(d),
        scratch_shapes=[pltpu.VMEM((2, TOPK_IN_GROUP, tm, d), F32), pltpu.SemaphoreType.DMA((2,))],
    )
    return pl.pallas_call(
        _combine_kernel,
        grid_spec=grid_spec,
        out_shape=jax.ShapeDtypeStruct((t, d), F32),
        compiler_params=_cparams("arbitrary"),
    )(dst, y_sorted, route, x, gate.reshape(n_batch, 1, d), ln_g.reshape(1, d), ln_b.reshape(1, d))


def _sort_plan(route, seq):
    t = route.shape[0]
    tm = MOE_TILE
    eid = route[:, :TOPK_IN_GROUP].astype(jnp.int32).reshape(-1)
    onehot = (eid[:, None] == jnp.arange(N_EXPERTS)[None, :]).astype(jnp.int32)
    before = jnp.cumsum(onehot, axis=0) - onehot
    rank = jnp.sum(before * onehot, axis=1)
    counts = jnp.sum(onehot, axis=0)
    tiles = (counts + tm - 1) // tm
    tile_end = jnp.cumsum(tiles)
    tile_start = tile_end - tiles
    dst = tile_start[eid] * tm + rank
    n_tiles = (t * TOPK_IN_GROUP) // tm + N_EXPERTS
    tile_ids = jnp.arange(n_tiles)
    tile_expert = jnp.sum((tile_ids[:, None] >= tile_end[None, :]).astype(jnp.int32), axis=1)
    tile_expert = jnp.minimum(tile_expert, N_EXPERTS - 1)
    n_used = tile_end[-1:].astype(jnp.int32)
    tok = jnp.arange(t * TOPK_IN_GROUP, dtype=jnp.int32) // TOPK_IN_GROUP
    src_tok = jnp.zeros((n_tiles * tm,), jnp.int32).at[dst].set(tok)
    row_batch = (src_tok // seq).reshape(-1, 1)
    return dst.astype(jnp.int32), tile_expert.astype(jnp.int32), src_tok, n_used, row_batch


def _moe_layer(x, scale, shift, gate, w_group, b_group, w_router, b_router, w1, w3, w2,
               ln_g, ln_b, seq):
    route = _router(x, scale, shift, w_group, b_group, w_router, b_router, seq)
    dst, tile_expert, src_tok, n_used, row_batch = _sort_plan(route, seq)
    y_sorted = _experts(x, scale, shift, tile_expert, src_tok, n_used, row_batch,
                        w1.astype(BF16), w3.astype(BF16), w2.astype(BF16))
    return _combine_ln(dst, y_sorted, route, x, gate, ln_g, ln_b, seq)


def _swa_layer(x, scale, shift, gate, tables, w_qkv, b_qkv, sinks, w_o, ln_g, ln_b, seq):
    qw = Q_HEADS * HEAD_DIM
    kvw = KV_HEADS * HEAD_DIM
    tn = 256
    qkv = _mod_proj(x, scale, shift, w_qkv.astype(BF16), b_qkv, tables, seq,
                    rope_lo=qw // tn, rope_hi=(qw + kvw) // tn, tn=tn)
    o = _band_attention(qkv, 0, qw // kvw, qw // kvw + 1, tables, seq, SWA_WINDOW, sinks=sinks)
    return _out_proj_ln([o], w_o.astype(BF16), x, gate, ln_g, ln_b, seq)


def _nsa_layer(x, scale, shift, gate, tables, w_in, pe_k, pe_v, phi_k1, phi_k2, phi_v1, phi_v2,
               w_o, ln_g, ln_b, seq):
    t, d = x.shape
    n_batch = t // seq
    qw = Q_HEADS * HEAD_DIM
    kvw = KV_HEADS * HEAD_DIM
    tn = 256
    cols = [qw + i * kvw for i in range(7)]
    w_q, w_kc, w_vc, w_ks, w_vs, w_kw, w_vw, w_g = jnp.split(w_in, cols, axis=1)
    w_g = w_g.reshape(d, Q_HEADS, 3).transpose(0, 2, 1).reshape(d, 3 * Q_HEADS)
    w_g = jnp.pad(w_g, ((0, 0), (0, tn - 3 * Q_HEADS)))
    w_all = jnp.concatenate([w_q, w_ks, w_kw, w_vs, w_vw, w_kc, w_vc, w_g], axis=1).astype(BF16)
    n_cols = w_all.shape[1]
    proj = _mod_proj(x, scale, shift, w_all, jnp.zeros((n_cols,), F32), tables, seq,
                     rope_lo=qw // tn, rope_hi=(qw + 2 * kvw) // tn, tn=tn)
    kv_blk = qw // kvw
    k_s, v_s, k_c, v_c = (proj[:, qw + i * kvw: qw + (i + 1) * kvw] for i in (0, 2, 4, 5))
    gate_blk = (qw + 6 * kvw) // LANES

    kc = _compress(k_c, pe_k, phi_k1, phi_k2, seq)
    vc = _compress(v_c, pe_v, phi_v1, phi_v2, seq)
    o_cmp, selb = _cmp_attention(proj, kc, vc, gate_blk, seq, gate_col=0)

    heads = lambda a: a.reshape(n_batch, seq, KV_HEADS, HEAD_DIM).transpose(0, 2, 1, 3)
    onehot = (jnp.arange(seq)[:, None] // SEL_BLOCK == jnp.arange(HEAD_DIM)[None, :]).astype(BF16)
    onehot = jnp.broadcast_to(onehot, (n_batch, KV_HEADS, seq, HEAD_DIM))
    ks_h, vs_h = heads(k_s), heads(v_s)
    ka_even = jnp.concatenate([ks_h, onehot], axis=-1)
    ka_odd = jnp.concatenate([onehot, ks_h], axis=-1)
    vs2 = jnp.concatenate([vs_h, vs_h], axis=-1)
    o_sel = _sel_attention(proj, selb, ka_even, ka_odd, vs2, tables, gate_blk, seq, gate_col=Q_HEADS)
    o_win = _band_attention(proj, 0, kv_blk + 1, kv_blk + 3, tables, seq, NSA_WINDOW,
                            gate_blk=gate_blk, gate_col=2 * Q_HEADS)
    return _out_proj_ln([o_cmp, o_sel, o_win], w_o.astype(BF16), x, gate, ln_g, ln_b, seq)


def kernel(x, c, positions, w_ada, b_ada, swa_w_qkv, swa_b_qkv, swa_sinks, swa_w_o, nsa_w_in,
           nsa_pe_k, nsa_pe_v, nsa_phi_k1, nsa_phi_k2, nsa_phi_v1, nsa_phi_v2, nsa_w_o,
           moe_w_group, moe_b_group, moe_w_router, moe_b_router, moe_w1, moe_w3, moe_w2,
           ln_t_g, ln_t_b, ln_c_g, ln_c_b):
    n_batch, seq, d = x.shape
    depth = w_ada.shape[0]
    xt = x.reshape(n_batch * seq, d)
    mod = _adaln_mod(c, w_ada, b_ada)
    tables = _rope_tables(positions)
    for i in range(depth):
        sh_t, sc_t, g_t, sh_c, sc_c, g_c = (mod[i, :, k * d:(k + 1) * d] for k in range(6))
        j = i // 2
        if i % 2 == 0:
            xt = _swa_layer(xt, sc_t, sh_t, g_t, tables, swa_w_qkv[j], swa_b_qkv[j], swa_sinks[j],
                            swa_w_o[j], ln_t_g[i], ln_t_b[i], seq)
        else:
            xt = _nsa_layer(xt, sc_t, sh_t, g_t, tables, nsa_w_in[j], nsa_pe_k[j], nsa_pe_v[j],
                            nsa_phi_k1[j], nsa_phi_k2[j], nsa_phi_v1[j], nsa_phi_v2[j], nsa_w_o[j],
                            ln_t_g[i], ln_t_b[i], seq)
        xt = _moe_layer(xt, sc_c, sh_c, g_c, moe_w_group[i], moe_b_group[i], moe_w_router[i],
                        moe_b_router[i], moe_w1[i], moe_w3[i], moe_w2[i], ln_c_g[i], ln_c_b[i], seq)
    return xt.reshape(n_batch, seq, d)
```

```python
import functools

import jax
import jax.numpy as jnp
from jax import lax
from jax.experimental import pallas as pl
from jax.experimental.pallas import tpu as pltpu

F32 = jnp.float32
BF16 = jnp.bfloat16

HEAD_DIM = 64
ROPE_DIM = HEAD_DIM // 4
ROPE_HALF = ROPE_DIM // 2
ROPE_THETA = 500000.0
Q_HEADS = 32
KV_HEADS = 4
GROUP = Q_HEADS // KV_HEADS
SWA_WINDOW = 128
NSA_WINDOW = 512
CMP_BLOCK = 32
CMP_STRIDE = 16
SEL_BLOCK = 64
SEL_TOPK = 16
FORCE_BONUS = 1e4
N_GROUPS = 4
EXPERTS_PER_GROUP = 4
N_EXPERTS = N_GROUPS * EXPERTS_PER_GROUP
TOPK_IN_GROUP = 2
DEPTH = 2
ALPHA = (2 * DEPTH) ** 0.25
LN_EPS = 1e-5
NEG_INF = -1e30
PICKED = -3e38

LANES = 128
ATTN_BLOCK = 128
VMEM_LIMIT = 56 * 1024 * 1024


def _cparams(*sem):
    return pltpu.CompilerParams(dimension_semantics=sem, vmem_limit_bytes=VMEM_LIMIT)


def _mod_kernel(cb_ref, w_ref, b_ref, o_ref, cs_ref, *, n_batch, tn):
    @pl.when((pl.program_id(0) == 0) & (pl.program_id(1) == 0))
    def _():
        c = cb_ref[...]
        cs_ref[...] = c * (1.0 / (1.0 + jnp.exp(-c)))

    for b in range(n_batch):
        cs = cs_ref[b]
        parts = []
        for g in range(tn // LANES):
            wg = w_ref[0, :, g * LANES:(g + 1) * LANES]
            parts.append(jnp.sum(wg * cs, axis=0, keepdims=True))
        o_ref[0, b:b + 1, :] = jnp.concatenate(parts, axis=-1) + b_ref[0]


def _adaln_mod(c, w_ada, b_ada):
    n_batch, d = c.shape
    depth, _, n = w_ada.shape
    tn = 512
    cb = jnp.broadcast_to(c[:, :, None], (n_batch, d, LANES))
    return pl.pallas_call(
        functools.partial(_mod_kernel, n_batch=n_batch, tn=tn),
        grid=(depth, n // tn),
        in_specs=[
            pl.BlockSpec((n_batch, d, LANES), lambda l, j: (0, 0, 0)),
            pl.BlockSpec((1, d, tn), lambda l, j: (l, 0, j)),
            pl.BlockSpec((1, 1, tn), lambda l, j: (l, 0, j)),
        ],
        out_specs=pl.BlockSpec((1, n_batch, tn), lambda l, j: (l, 0, j)),
        out_shape=jax.ShapeDtypeStruct((depth, n_batch, n), F32),
        scratch_shapes=[pltpu.VMEM((n_batch, d, LANES), F32)],
        compiler_params=_cparams("arbitrary", "arbitrary"),
    )(cb, w_ada, b_ada.reshape(depth, 1, n))


def _rope_table_kernel(pos_ref, inv_ref, c_ref, s1_ref, s2_ref):
    ang = pos_ref[...] * inv_ref[...]
    d = lax.broadcasted_iota(jnp.int32, ang.shape, 1) % HEAD_DIM
    cos = jnp.cos(ang)
    sin = jnp.sin(ang)
    c_ref[...] = jnp.where(d < ROPE_DIM, cos, 1.0)
    s1_ref[...] = jnp.where(d < ROPE_HALF, -sin, 0.0)
    s2_ref[...] = jnp.where((d >= ROPE_HALF) & (d < ROPE_DIM), sin, 0.0)


def _rope_tables(positions):
    t = positions.size
    tm = 1024
    inv_freq = ROPE_THETA ** (-jnp.arange(0, ROPE_DIM, 2, dtype=F32) / ROPE_DIM)
    lane = jnp.arange(LANES) % HEAD_DIM
    inv_lane = inv_freq[lane % ROPE_HALF].reshape(1, LANES)
    pos = positions.astype(F32).reshape(t, 1)
    spec = pl.BlockSpec((tm, LANES), lambda i: (i, 0))
    return pl.pallas_call(
        _rope_table_kernel,
        grid=(t // tm,),
        in_specs=[pl.BlockSpec((tm, 1), lambda i: (i, 0)),
                  pl.BlockSpec((1, LANES), lambda i: (0, 0))],
        out_specs=[spec, spec, spec],
        out_shape=[jax.ShapeDtypeStruct((t, LANES), F32)] * 3,
        compiler_params=_cparams("arbitrary"),
    )(pos, inv_lane)


def _apply_rope(x, c, s1, s2):
    reps = x.shape[-1] // LANES
    if reps > 1:
        c = jnp.concatenate([c] * reps, axis=-1)
        s1 = jnp.concatenate([s1] * reps, axis=-1)
        s2 = jnp.concatenate([s2] * reps, axis=-1)
    n = x.shape[-1]
    up = pltpu.roll(x, n - ROPE_HALF, 1)
    down = pltpu.roll(x, ROPE_HALF, 1)
    return x * c + up * s1 + down * s2


def _proj_kernel(x_ref, sc_ref, sh_ref, w_ref, b_ref, c_ref, s1_ref, s2_ref, o_ref, h_ref,
                 *, rope_lo, rope_hi):
    j = pl.program_id(1)

    @pl.when(j == 0)
    def _():
        h_ref[...] = (x_ref[...] * (1.0 + sc_ref[0]) + sh_ref[0]).astype(BF16)

    acc = jnp.dot(h_ref[...], w_ref[...], preferred_element_type=F32) + b_ref[...]
    if rope_hi > rope_lo:
        is_rope = (j >= rope_lo) & (j < rope_hi)

        @pl.when(is_rope)
        def _():
            o_ref[...] = _apply_rope(acc, c_ref[...], s1_ref[...], s2_ref[...]).astype(o_ref.dtype)

        @pl.when(jnp.logical_not(is_rope))
        def _():
            o_ref[...] = acc.astype(o_ref.dtype)
    else:
        o_ref[...] = acc.astype(o_ref.dtype)


def _mod_proj(x, scale, shift, w, bias, tables, seq, *, rope_lo=0, rope_hi=0, tn=256, tm=1024,
              out_dtype=BF16):
    t, d = x.shape
    n = w.shape[1]
    n_batch = scale.shape[0]
    tm = min(tm, seq)
    per_batch = seq // tm
    c_tab, s1_tab, s2_tab = tables
    vec = pl.BlockSpec((1, 1, d), lambda i, j: (i // per_batch, 0, 0))
    tab = pl.BlockSpec((tm, LANES), lambda i, j: (i, 0))
    return pl.pallas_call(
        functools.partial(_proj_kernel, rope_lo=rope_lo, rope_hi=rope_hi),
        grid=(t // tm, n // tn),
        in_specs=[
            pl.BlockSpec((tm, d), lambda i, j: (i, 0)),
            vec, vec,
            pl.BlockSpec((d, tn), lambda i, j: (0, j)),
            pl.BlockSpec((1, tn), lambda i, j: (0, j)),
            tab, tab, tab,
        ],
        out_specs=pl.BlockSpec((tm, tn), lambda i, j: (i, j)),
        out_shape=jax.ShapeDtypeStruct((t, n), out_dtype),
        scratch_shapes=[pltpu.VMEM((tm, d), BF16)],
        compiler_params=_cparams("arbitrary", "arbitrary"),
    )(x, scale.reshape(n_batch, 1, d), shift.reshape(n_batch, 1, d), w, bias.reshape(1, n),
      c_tab, s1_tab, s2_tab)


def _scaled_q(q_ref, tables=None):
    q = q_ref[...].astype(F32)
    if tables is not None:
        q = _apply_rope(q, *(t[...] for t in tables))
    return q * (HEAD_DIM ** -0.5)


def _split_even_odd(q):
    even = (lax.broadcasted_iota(jnp.int32, q.shape, 1) % LANES) < HEAD_DIM
    return jnp.where(even, q, 0.0).astype(BF16), jnp.where(even, 0.0, q).astype(BF16)


def _stack_heads(q_even, q_odd, first_head):
    parts = []
    for g in range(GROUP):
        head = first_head + g
        src = q_even if head % 2 == 0 else q_odd
        parts.append(src[:, (head // 2) * LANES:(head // 2 + 1) * LANES])
    return jnp.concatenate(parts, axis=0)


def _twice(x):
    return jnp.concatenate([x, x], axis=-1)


def _merge_pairs(o, rows, scale):
    even = lax.broadcasted_iota(jnp.int32, (rows, LANES), 1) < HEAD_DIM
    out = []
    for g in range(0, GROUP, 2):
        a = o[g * rows:(g + 1) * rows] * scale[g]
        b = o[(g + 1) * rows:(g + 2) * rows] * scale[g + 1]
        out.append(jnp.where(even, a, b))
    return jnp.concatenate(out, axis=-1)


def _sigmoid(x):
    return 1.0 / (1.0 + jnp.exp(-x))


def _band_attn_kernel(*refs, window, use_sinks, use_gate, gate_col):
    it = iter(refs)
    q_ref, k_ref, v_ref, c_ref, s1_ref, s2_ref = (next(it) for _ in range(6))
    sink_ref = next(it) if use_sinks else None
    gate_ref = next(it) if use_gate else None
    o_ref = next(it)

    n = pl.program_id(1)
    rows = ATTN_BLOCK
    halo = -(-window // rows) * rows
    span = rows + halo
    start = jnp.maximum(n * rows - halo, 0)
    start = pl.multiple_of(start, rows)
    q_even, q_odd = _split_even_odd(_scaled_q(q_ref, (c_ref, s1_ref, s2_ref)))
    k = k_ref[pl.ds(start, span), :]
    v = v_ref[pl.ds(start, span), :]
    qpos = n * rows + lax.broadcasted_iota(jnp.int32, (rows, span), 0)
    kpos = start + lax.broadcasted_iota(jnp.int32, (rows, span), 1)
    rel = qpos - kpos
    mask = (rel >= 0) & (rel < window)
    if use_gate:
        gate = _sigmoid(gate_ref[...].astype(F32))

    for j in range(KV_HEADS):
        k2 = _twice(k[:, j * HEAD_DIM:(j + 1) * HEAD_DIM])
        v2 = _twice(v[:, j * HEAD_DIM:(j + 1) * HEAD_DIM])
        q8 = _stack_heads(q_even, q_odd, j * GROUP)
        s = lax.dot_general(q8, k2, (((1,), (1,)), ((), ())), preferred_element_type=F32)
        ps, scale = [], []
        for g in range(GROUP):
            sg = jnp.where(mask, s[g * rows:(g + 1) * rows], NEG_INF)
            m = jnp.max(sg, axis=-1, keepdims=True)
            if use_sinks:
                sink = sink_ref[j * GROUP + g]
                m = jnp.maximum(m, sink)
            e = jnp.exp(sg - m)
            den = jnp.sum(e, axis=-1, keepdims=True)
            if use_sinks:
                den = den + jnp.exp(sink - m)
            inv = 1.0 / den
            if use_gate:
                col = gate_col + j * GROUP + g
                inv = inv * gate[:, col:col + 1]
            ps.append(e.astype(BF16))
            scale.append(inv)
        o = jnp.dot(jnp.concatenate(ps, axis=0), v2, preferred_element_type=F32)
        o_ref[:, j * GROUP * HEAD_DIM:(j + 1) * GROUP * HEAD_DIM] = (
            _merge_pairs(o, rows, scale).astype(o_ref.dtype))


def _band_attention(proj, q_blk, k_blk, v_blk, tables, seq, window, sinks=None, gate_blk=None,
                    gate_col=0):
    t = proj.shape[0]
    n_batch = t // seq
    nb = seq // ATTN_BLOCK
    qw = Q_HEADS * HEAD_DIM
    kvw = KV_HEADS * HEAD_DIM
    row = lambda w, c: pl.BlockSpec((ATTN_BLOCK, w), lambda b, i: (b * nb + i, c))
    res = lambda c: pl.BlockSpec((seq, kvw), lambda b, i: (b, c))
    in_specs = [row(qw, q_blk), res(k_blk), res(v_blk), row(LANES, 0), row(LANES, 0), row(LANES, 0)]
    args = [proj, proj, proj, *tables]
    if sinks is not None:
        in_specs.append(pl.BlockSpec(memory_space=pltpu.SMEM))
        args.append(sinks)
    if gate_blk is not None:
        in_specs.append(row(LANES, gate_blk))
        args.append(proj)
    return pl.pallas_call(
        functools.partial(_band_attn_kernel, window=window, use_sinks=sinks is not None,
                          use_gate=gate_blk is not None, gate_col=gate_col),
        grid=(n_batch, nb),
        in_specs=in_specs,
        out_specs=row(qw, 0),
        out_shape=jax.ShapeDtypeStruct((t, qw), BF16),
        compiler_params=_cparams("arbitrary", "arbitrary"),
    )(*args)


def _layer_norm(y, g, b):
    mu = jnp.mean(y, axis=-1, keepdims=True)
    yc = y - mu
    var = jnp.mean(yc * yc, axis=-1, keepdims=True)
    return yc * lax.rsqrt(var + LN_EPS) * g + b


def _out_proj_kernel(*refs, n_parts):
    o_parts = refs[:n_parts]
    w_ref, x_ref, gate_ref, g_ref, b_ref, out_ref, out_tm_ref = refs[n_parts:]
    if n_parts == 1:
        o = o_parts[0][...]
    else:
        acc = o_parts[0][...].astype(F32)
        for r in o_parts[1:]:
            acc = acc + r[...].astype(F32)
        o = acc.astype(BF16)
    mix = jnp.dot(o, w_ref[...], preferred_element_type=F32)
    y = ALPHA * x_ref[...] + gate_ref[0] * mix
    res = _layer_norm(y, g_ref[...], b_ref[...])
    out_ref[...] = res
    _store_token_major(out_tm_ref, res)


def _out_proj_ln(o_parts, w_o, x, gate, ln_g, ln_b, seq, tm=256):
    t, d = x.shape
    n = d // LANES
    n_batch = gate.shape[0]
    per_batch = seq // tm
    k = w_o.shape[0]
    row_o = pl.BlockSpec((tm, k), lambda i: (i, 0))
    row_x = pl.BlockSpec((tm, d), lambda i: (i, 0))
    vec = pl.BlockSpec((1, d), lambda i: (0, 0))
    return pl.pallas_call(
        functools.partial(_out_proj_kernel, n_parts=len(o_parts)),
        grid=(t // tm,),
        in_specs=[row_o] * len(o_parts) + [
            pl.BlockSpec((k, d), lambda i: (0, 0)),
            row_x,
            pl.BlockSpec((1, 1, d), lambda i: (i // per_batch, 0, 0)),
            vec, vec],
        out_specs=[row_x, pl.BlockSpec((tm * n, LANES), lambda i: (i, 0))],
        out_shape=[jax.ShapeDtypeStruct((t, d), F32), jax.ShapeDtypeStruct((t * n, LANES), F32)],
        compiler_params=_cparams("arbitrary"),
    )(*o_parts, w_o, x, gate.reshape(n_batch, 1, d), ln_g.reshape(1, d), ln_b.reshape(1, d))


def _compress_kernel(x_ref, pe_ref, w1_ref, w1a_ref, w1b_ref, w2_ref, o_ref):
    x = x_ref[0]
    a = jnp.dot(x, w1a_ref[...], preferred_element_type=F32)
    b = jnp.dot(x, w1b_ref[...], preferred_element_type=F32)
    nc = a.shape[0]
    b_next = pltpu.roll(b, nc - 1, 0)
    pe_term = jnp.dot(pe_ref[...], w1_ref[...], preferred_element_type=F32)[0:1]
    hid = a + b_next + jnp.concatenate([pe_term] * KV_HEADS, axis=-1)
    hid = hid * (1.0 / (1.0 + jnp.exp(-hid)))
    o_ref[0] = jnp.dot(hid.astype(BF16), w2_ref[...], preferred_element_type=F32).astype(o_ref.dtype)


def _block_diag_heads(w):
    p, d, n = w.shape
    eye = jnp.eye(KV_HEADS, dtype=w.dtype)
    big = w[:, None, :, None, :] * eye[None, :, None, :, None]
    return big.reshape(p * KV_HEADS * d, KV_HEADS * n)


def _compress(xc, pe, w1, w2, seq):
    t, kvw = xc.shape
    n_batch = t // seq
    nch = seq // CMP_STRIDE
    hidden = w1.shape[1]
    x = xc.reshape(n_batch, nch, CMP_STRIDE * kvw)
    w1r = w1.reshape(CMP_BLOCK, HEAD_DIM, hidden)
    w1a = _block_diag_heads(w1r[:CMP_STRIDE]).astype(BF16)
    w1b = _block_diag_heads(w1r[CMP_STRIDE:]).astype(BF16)
    eye = jnp.eye(KV_HEADS, dtype=w2.dtype)
    w2d = (w2[None, :, None, :] * eye[:, None, :, None]).reshape(KV_HEADS * hidden, kvw).astype(BF16)
    pe8 = jnp.broadcast_to(pe.reshape(1, CMP_BLOCK * HEAD_DIM), (8, CMP_BLOCK * HEAD_DIM)).astype(BF16)
    full = lambda a: pl.BlockSpec(a.shape, lambda b: (0,) * a.ndim)
    w1b16 = w1.astype(BF16)
    return pl.pallas_call(
        _compress_kernel,
        grid=(n_batch,),
        in_specs=[pl.BlockSpec((1, nch, CMP_STRIDE * kvw), lambda b: (b, 0, 0)),
                  full(pe8), full(w1b16), full(w1a), full(w1b), full(w2d)],
        out_specs=pl.BlockSpec((1, nch, kvw), lambda b: (b, 0, 0)),
        out_shape=jax.ShapeDtypeStruct((n_batch, nch, kvw), BF16),
        compiler_params=_cparams("arbitrary"),
    )(x, pe8, w1b16, w1a, w1b, w2d)


def _topk_bias(score, top_k):
    nblk, rows = score.shape
    sub = lax.broadcasted_iota(jnp.int32, (8, rows), 0)
    chunks = [score[c * 8:(c + 1) * 8] for c in range(nblk // 8)]
    counts = [jnp.zeros((8, rows), F32) for _ in chunks]
    for sp in range(nblk):
        row = score[sp:sp + 1]
        for c, chunk in enumerate(chunks):
            ge = jnp.where(row >= chunk, 1.0, 0.0)
            gt = jnp.where(row > chunk, 1.0, 0.0)
            if sp < c * 8:
                beats = ge
            elif sp >= (c + 1) * 8:
                beats = gt
            else:
                beats = jnp.where(sub > sp - c * 8, ge, gt)
            counts[c] = counts[c] + beats
    return jnp.concatenate([jnp.where(cnt < top_k, 0.0, NEG_INF) for cnt in counts], axis=0)


def _cmp_attn_kernel(q_ref, kc_ref, vc_ref, ovt_ref, gate_ref, o_ref, sel_ref, *, gate_col, top_k):
    n = pl.program_id(1)
    rows = ATTN_BLOCK
    nc = kc_ref.shape[1]
    nblk = ovt_ref.shape[0]
    q_even, q_odd = _split_even_odd(_scaled_q(q_ref))
    kc = kc_ref[0]
    vc = vc_ref[0]
    t = n * rows + lax.broadcasted_iota(jnp.int32, (rows, nc), 0)
    cmp_end = lax.broadcasted_iota(jnp.int32, (rows, nc), 1) * CMP_STRIDE + (CMP_BLOCK - 1)
    valid = cmp_end <= t
    has_valid = n * rows + lax.broadcasted_iota(jnp.int32, (rows, 1), 0) >= CMP_BLOCK - 1
    gate = _sigmoid(gate_ref[...].astype(F32))

    ts = n * rows + lax.broadcasted_iota(jnp.int32, (nblk, rows), 1)
    blk = lax.broadcasted_iota(jnp.int32, (nblk, rows), 0)
    cur = lax.shift_right_arithmetic(ts, SEL_BLOCK.bit_length() - 1)
    causal = blk * SEL_BLOCK <= ts
    forced = (blk == 0) | (blk == cur) | (blk == cur - 1)
    bonus = jnp.where(forced, FORCE_BONUS, 0.0)

    for j in range(KV_HEADS):
        k2 = _twice(kc[:, j * HEAD_DIM:(j + 1) * HEAD_DIM])
        v2 = _twice(vc[:, j * HEAD_DIM:(j + 1) * HEAD_DIM])
        q8 = _stack_heads(q_even, q_odd, j * GROUP)
        s = lax.dot_general(q8, k2, (((1,), (1,)), ((), ())), preferred_element_type=F32)
        ps, scale, psum = [], [], None
        for g in range(GROUP):
            sg = jnp.where(valid, s[g * rows:(g + 1) * rows], NEG_INF)
            m = jnp.max(sg, axis=-1, keepdims=True)
            e = jnp.exp(sg - m)
            den = jnp.sum(e, axis=-1, keepdims=True)
            inv = jnp.where(has_valid, 1.0 / den, 0.0)
            pn = e * inv
            psum = pn if psum is None else psum + pn
            col = gate_col + j * GROUP + g
            ps.append(e.astype(BF16))
            scale.append(inv * gate[:, col:col + 1])
        o = jnp.dot(jnp.concatenate(ps, axis=0), v2, preferred_element_type=F32)
        o_ref[:, j * GROUP * HEAD_DIM:(j + 1) * GROUP * HEAD_DIM] = (
            _merge_pairs(o, rows, scale).astype(o_ref.dtype))

        imp_t = lax.dot_general(ovt_ref[...], psum, (((1,), (1,)), ((), ())),
                                preferred_element_type=F32,
                                precision=lax.Precision.HIGHEST)
        score = jnp.where(causal, imp_t + bonus, NEG_INF)
        bias_t = _topk_bias(score, top_k)
        sel_ref[0, j] = jnp.concatenate([bias_t, bias_t], axis=0).T.astype(sel_ref.dtype)


def _cmp_attention(proj, kc, vc, gate_blk, seq, gate_col):
    t = proj.shape[0]
    n_batch = t // seq
    nb = seq // ATTN_BLOCK
    nc = kc.shape[1]
    nsel = seq // SEL_BLOCK
    assert nsel <= HEAD_DIM
    qw = Q_HEADS * HEAD_DIM
    top_k = min(SEL_TOPK, nsel)
    c0 = jnp.arange(nc)[None, :] * CMP_STRIDE
    s0 = jnp.arange(HEAD_DIM)[:, None] * SEL_BLOCK
    ov = jnp.clip(jnp.minimum(c0 + CMP_BLOCK, s0 + SEL_BLOCK) - jnp.maximum(c0, s0), 0)
    overlap_t = ov.astype(F32) / CMP_BLOCK
    row = lambda w, c: pl.BlockSpec((ATTN_BLOCK, w), lambda b, i: (b * nb + i, c))
    res = pl.BlockSpec((1, nc, kc.shape[2]), lambda b, i: (b, 0, 0))
    return pl.pallas_call(
        functools.partial(_cmp_attn_kernel, gate_col=gate_col, top_k=top_k),
        grid=(n_batch, nb),
        in_specs=[row(qw, 0), res, res,
                  pl.BlockSpec((HEAD_DIM, nc), lambda b, i: (0, 0)), row(LANES, gate_blk)],
        out_specs=[row(qw, 0),
                   pl.BlockSpec((1, KV_HEADS, ATTN_BLOCK, LANES), lambda b, i: (b, 0, i, 0))],
        out_shape=[jax.ShapeDtypeStruct((t, qw), BF16),
                   jax.ShapeDtypeStruct((n_batch, KV_HEADS, seq, LANES), BF16)],
        compiler_params=_cparams("arbitrary", "arbitrary"),
    )(proj, kc, vc, overlap_t, proj)


SEL_TILE = 512


def _sel_attn_kernel(q_ref, sel_ref, kae_ref, kao_ref, v_ref, c_ref, s1_ref, s2_ref, gate_ref, o_ref,
                     m_ref, l_ref, acc_ref, *, gate_col):
    j = pl.program_id(1)
    n = pl.program_id(2)
    rows = ATTN_BLOCK
    tk = SEL_TILE
    pairs = GROUP // 2
    q = _scaled_q(q_ref, (c_ref, s1_ref, s2_ref))
    bias2 = sel_ref[0, 0].astype(F32)
    low = lax.broadcasted_iota(jnp.int32, (rows, LANES), 1) < HEAD_DIM
    lhs = []
    for parity in range(2):
        parts = []
        for i in range(pairs):
            slab = q[:, i * LANES:(i + 1) * LANES]
            parts.append(jnp.where(low, slab, bias2) if parity == 0 else jnp.where(low, bias2, slab))
        lhs.append(jnp.concatenate(parts, axis=0).astype(BF16))
    ka_refs = (kae_ref, kao_ref)

    m_ref[...] = jnp.full(m_ref.shape, NEG_INF, F32)
    l_ref[...] = jnp.zeros(l_ref.shape, F32)
    acc_ref[...] = jnp.zeros(acc_ref.shape, F32)

    def tile(kt, mask_bias):
        off = pl.multiple_of(kt * tk, tk)
        v2 = v_ref[0, 0, pl.ds(off, tk), :]
        for parity in range(2):
            ka = ka_refs[parity][0, 0, pl.ds(off, tk), :]
            s = lax.dot_general(lhs[parity], ka, (((1,), (1,)), ((), ())),
                                preferred_element_type=F32)
            if mask_bias is not None:
                s = s + mask_bias
            chunks = [s[:, c * LANES:(c + 1) * LANES] for c in range(tk // LANES)]
            rmax = chunks[0]
            for ch in chunks[1:]:
                rmax = jnp.maximum(rmax, ch)
            m_old = m_ref[parity]
            m_new = jnp.maximum(m_old, jnp.max(rmax, axis=-1, keepdims=True))
            alpha = jnp.exp(m_old - m_new)
            ps = [jnp.exp(ch - m_new) for ch in chunks]
            lsum = ps[0]
            for x in ps[1:]:
                lsum = lsum + x
            l_ref[parity] = l_ref[parity] * alpha + lsum
            p = jnp.concatenate([x.astype(BF16) for x in ps], axis=-1)
            acc_ref[parity] = acc_ref[parity] * alpha + jnp.dot(p, v2, preferred_element_type=F32)
            m_ref[parity] = m_new

    def body(kt, carry):
        tile(kt, None)
        return carry

    n_full = (n * rows) // tk
    lax.fori_loop(0, n_full, body, 0)
    qpos = n * rows + lax.broadcasted_iota(jnp.int32, (rows, tk), 0)
    kpos = n_full * tk + lax.broadcasted_iota(jnp.int32, (rows, tk), 1)
    causal = jnp.where(kpos <= qpos, 0.0, NEG_INF)
    tile(n_full, jnp.concatenate([causal] * pairs, axis=0))

    gate = _sigmoid(gate_ref[...].astype(F32))
    lane = lax.broadcasted_iota(jnp.int32, gate.shape, 1)
    outs = []
    for i in range(pairs):
        scaled = []
        for parity in range(2):
            col = gate_col + j * GROUP + 2 * i + parity
            gcol = jnp.sum(jnp.where(lane == col, gate, 0.0), axis=-1, keepdims=True)
            l = jnp.sum(l_ref[parity, i * rows:(i + 1) * rows], axis=-1, keepdims=True)
            scaled.append(acc_ref[parity, i * rows:(i + 1) * rows] * (gcol / l))
        outs.append(jnp.where(low, scaled[0], scaled[1]))
    o_ref[...] = jnp.concatenate(outs, axis=-1).astype(o_ref.dtype)


def _sel_attention(proj, selb, ka_even, ka_odd, v2, tables, gate_blk, seq, gate_col):
    t = proj.shape[0]
    n_batch = t // seq
    nb = seq // ATTN_BLOCK
    gw = GROUP * HEAD_DIM
    pairs = GROUP // 2
    row = lambda c: pl.BlockSpec((ATTN_BLOCK, LANES), lambda b, j, i: (b * nb + i, c))
    res = pl.BlockSpec((1, 1, seq, LANES), lambda b, j, i: (b, j, 0, 0))
    acc = pltpu.VMEM((2, pairs * ATTN_BLOCK, LANES), F32)
    return pl.pallas_call(
        functools.partial(_sel_attn_kernel, gate_col=gate_col),
        grid=(n_batch, KV_HEADS, nb),
        in_specs=[
            pl.BlockSpec((ATTN_BLOCK, gw), lambda b, j, i: (b * nb + i, j)),
            pl.BlockSpec((1, 1, ATTN_BLOCK, LANES), lambda b, j, i: (b, j, i, 0)),
            res, res, res, row(0), row(0), row(0), row(gate_blk)],
        out_specs=pl.BlockSpec((ATTN_BLOCK, gw), lambda b, j, i: (b * nb + i, j)),
        out_shape=jax.ShapeDtypeStruct((t, Q_HEADS * HEAD_DIM), BF16),
        scratch_shapes=[acc, acc, acc],
        compiler_params=_cparams("arbitrary", "arbitrary", "arbitrary"),
    )(proj, selb, ka_even, ka_odd, v2, *tables, proj)


def _router_kernel(x_ref, sc_ref, sh_ref, w_ref, b_ref, o_ref):
    h = x_ref[...] * (1.0 + sc_ref[0]) + sh_ref[0]
    logits = jnp.dot(h, w_ref[...], preferred_element_type=F32,
                     precision=lax.Precision.HIGHEST) + b_ref[...]
    lane = lax.broadcasted_iota(jnp.int32, logits.shape, 1)
    big = 1 << 20

    gmask = lane < N_GROUPS
    lg = jnp.where(gmask, logits, -jnp.inf)
    eg = jnp.exp(lg - jnp.max(lg, axis=-1, keepdims=True))
    pg = eg / jnp.sum(eg, axis=-1, keepdims=True)
    g_prob = jnp.max(pg, axis=-1, keepdims=True)
    g_idx = jnp.min(jnp.where((pg == g_prob) & gmask, lane, big), axis=-1, keepdims=True)

    lo = N_GROUPS + g_idx * EXPERTS_PER_GROUP
    emask = (lane >= lo) & (lane < lo + EXPERTS_PER_GROUP)
    le = jnp.where(emask, logits, -jnp.inf)
    ee = jnp.exp(le - jnp.max(le, axis=-1, keepdims=True))
    pe = jnp.where(emask, ee / jnp.sum(ee, axis=-1, keepdims=True), -1.0)
    p1 = jnp.max(pe, axis=-1, keepdims=True)
    i1 = jnp.min(jnp.where(pe == p1, lane, big), axis=-1, keepdims=True)
    pe2 = jnp.where(lane == i1, -1.0, pe)
    p2 = jnp.max(pe2, axis=-1, keepdims=True)
    i2 = jnp.min(jnp.where(pe2 == p2, lane, big), axis=-1, keepdims=True)
    tot = p1 + p2
    w1 = g_prob * (p1 / tot)
    w2 = g_prob * (p2 / tot)
    e1 = (i1 - N_GROUPS).astype(F32)
    e2 = (i2 - N_GROUPS).astype(F32)
    o_ref[...] = jnp.where(lane == 0, e1, jnp.where(lane == 1, e2,
                           jnp.where(lane == 2, w1, jnp.where(lane == 3, w2, 0.0))))


def _router(x, scale, shift, w_group, b_group, w_router, b_router, seq, tm=512):
    t, d = x.shape
    n_batch = scale.shape[0]
    per_batch = seq // tm
    w = jnp.zeros((d, LANES), F32).at[:, :N_GROUPS].set(w_group)
    w = w.at[:, N_GROUPS:N_GROUPS + N_EXPERTS].set(w_router)
    b = jnp.zeros((1, LANES), F32).at[0, :N_GROUPS].set(b_group)
    b = b.at[0, N_GROUPS:N_GROUPS + N_EXPERTS].set(b_router)
    vec = pl.BlockSpec((1, 1, d), lambda i: (i // per_batch, 0, 0))
    return pl.pallas_call(
        _router_kernel,
        grid=(t // tm,),
        in_specs=[pl.BlockSpec((tm, d), lambda i: (i, 0)), vec, vec,
                  pl.BlockSpec((d, LANES), lambda i: (0, 0)),
                  pl.BlockSpec((1, LANES), lambda i: (0, 0))],
        out_specs=pl.BlockSpec((tm, LANES), lambda i: (i, 0)),
        out_shape=jax.ShapeDtypeStruct((t, LANES), F32),
        compiler_params=_cparams("arbitrary"),
    )(x, scale.reshape(n_batch, 1, d), shift.reshape(n_batch, 1, d), w, b)


MOE_TILE = 256


def _store_token_major(ref, val):
    rows, n = val.shape[0], val.shape[1] // LANES
    for a in range(n):
        ref[pl.ds(a, rows, stride=n), :] = val[:, a * LANES:(a + 1) * LANES]


def _load_token_major(ref, rows, n):
    return jnp.concatenate([ref[pl.ds(a, rows, stride=n), :] for a in range(n)], axis=-1)


def _expert_kernel(te_ref, src_ref, nt_ref, x_hbm, rowb_ref, sc_ref, sh_ref, w1_ref, w3_ref, w2_ref,
                   o_ref, xbuf, w1b, w3b, w2b, sem, *, n_batch):
    i = pl.program_id(0)
    n_used = nt_ref[0]
    tm = MOE_TILE
    n = sc_ref.shape[-1] // LANES

    def row_copy(tile_idx, slot, r):
        tok = src_ref[tile_idx * tm + r]
        return pltpu.make_async_copy(x_hbm.at[pl.ds(pl.multiple_of(tok * n, n), n), :],
                                     xbuf.at[slot, pl.ds(pl.multiple_of(r * n, n), n), :],
                                     sem.at[slot])

    def start_tile(tile_idx, slot):
        def body(r, c):
            row_copy(tile_idx, slot, r).start()
            return c
        lax.fori_loop(0, tm, body, 0, unroll=8)

    def wait_tile(slot):
        pltpu.make_async_copy(x_hbm.at[pl.ds(0, tm * n), :], xbuf.at[slot], sem.at[slot]).wait()

    @pl.when((i == 0) & (n_used > 0))
    def _():
        start_tile(0, 0)

    @pl.when(i + 1 < n_used)
    def _():
        start_tile(i + 1, (i + 1) % 2)

    new_expert = (i == 0) | (te_ref[i] != te_ref[jnp.maximum(i - 1, 0)])

    @pl.when((i < n_used) & new_expert)
    def _():
        w1b[...] = w1_ref[0].astype(BF16)
        w3b[...] = w3_ref[0].astype(BF16)
        w2b[...] = w2_ref[0].astype(BF16)

    @pl.when(i < n_used)
    def _():
        slot = i % 2
        wait_tile(slot)
        x = _load_token_major(xbuf.at[slot], tm, n)
        rowb = rowb_ref[...]
        h = x * (1.0 + sc_ref[0]) + sh_ref[0]
        for b in range(1, n_batch):
            h = jnp.where(rowb == b, x * (1.0 + sc_ref[b]) + sh_ref[b], h)
        hb = h.astype(BF16)
        a = jnp.dot(hb, w1b[...], preferred_element_type=F32)
        g = jnp.dot(hb, w3b[...], preferred_element_type=F32)
        he = (a * _sigmoid(a) * g).astype(BF16)
        _store_token_major(o_ref, jnp.dot(he, w2b[...], preferred_element_type=F32))

    @pl.when(i >= n_used)
    def _():
        o_ref[...] = jnp.zeros(o_ref.shape, o_ref.dtype)


def _experts(x_tm, scale, shift, tile_expert, src_tok, n_used, row_batch, w1, w3, w2):
    n_batch, d = scale.shape
    n = d // LANES
    n_tiles = tile_expert.shape[0]
    de = w1.shape[2]
    tm = MOE_TILE
    grid_spec = pltpu.PrefetchScalarGridSpec(
        num_scalar_prefetch=3,
        grid=(n_tiles,),
        in_specs=[
            pl.BlockSpec(memory_space=pl.ANY),
            pl.BlockSpec((tm, 1), lambda i, te, src, nt: (i, 0)),
            pl.BlockSpec((n_batch, 1, d), lambda i, te, src, nt: (0, 0, 0)),
            pl.BlockSpec((n_batch, 1, d), lambda i, te, src, nt: (0, 0, 0)),
            pl.BlockSpec((1, d, de), lambda i, te, src, nt: (te[i], 0, 0)),
            pl.BlockSpec((1, d, de), lambda i, te, src, nt: (te[i], 0, 0)),
            pl.BlockSpec((1, de, d), lambda i, te, src, nt: (te[i], 0, 0)),
        ],
        out_specs=pl.BlockSpec((tm * n, LANES), lambda i, te, src, nt: (i, 0)),
        scratch_shapes=[pltpu.VMEM((2, tm * n, LANES), F32),
                        pltpu.VMEM((d, de), BF16), pltpu.VMEM((d, de), BF16),
                        pltpu.VMEM((de, d), BF16), pltpu.SemaphoreType.DMA((2,))],
    )
    return pl.pallas_call(
        functools.partial(_expert_kernel, n_batch=n_batch),
        grid_spec=grid_spec,
        out_shape=jax.ShapeDtypeStruct((n_tiles * tm * n, LANES), F32),
        compiler_params=_cparams("arbitrary"),
    )(tile_expert, src_tok, n_used, x_tm, row_batch, scale.reshape(n_batch, 1, d),
      shift.reshape(n_batch, 1, d), w1, w3, w2)


COMBINE_TILE = 256


def _combine_kernel(dst_ref, y_hbm, route_ref, x_ref, gate_ref, g_ref, b_ref, out_ref, ybuf, sem):
    i = pl.program_id(0)
    n_steps = pl.num_programs(0)
    tm = COMBINE_TILE

    n = x_ref.shape[-1] // LANES

    def row_copy(step, slot, k, r):
        pos = dst_ref[(step * tm + r) * TOPK_IN_GROUP + k]
        return pltpu.make_async_copy(
            y_hbm.at[pl.ds(pl.multiple_of(pos * n, n), n), :],
            ybuf.at[slot, pl.ds(pl.multiple_of((k * tm + r) * n, n), n), :], sem.at[slot])

    def start_step(step, slot):
        def body(r, c):
            for k in range(TOPK_IN_GROUP):
                row_copy(step, slot, k, r).start()
            return c
        lax.fori_loop(0, tm, body, 0, unroll=4)

    def wait_step(slot):
        pltpu.make_async_copy(y_hbm.at[pl.ds(0, TOPK_IN_GROUP * tm * n), :], ybuf.at[slot],
                              sem.at[slot]).wait()

    @pl.when(i == 0)
    def _():
        start_step(0, 0)

    @pl.when(i + 1 < n_steps)
    def _():
        start_step(i + 1, (i + 1) % 2)

    slot = i % 2
    wait_step(slot)
    route = route_ref[...]
    lane = lax.broadcasted_iota(jnp.int32, route.shape, 1)
    mix = None
    for k in range(TOPK_IN_GROUP):
        wk = jnp.sum(jnp.where(lane == TOPK_IN_GROUP + k, route, 0.0), axis=-1, keepdims=True)
        term = wk * _load_token_major(ybuf.at[slot, pl.ds(k * tm * n, tm * n), :], tm, n)
        mix = term if mix is None else mix + term
    y = ALPHA * x_ref[...] + gate_ref[0] * mix
    out_ref[...] = _layer_norm(y, g_ref[...], b_ref[...])


def _combine_ln(dst, y_sorted, route, x, gate, ln_g, ln_b, seq):
    t, d = x.shape
    n_batch = gate.shape[0]
    tm = COMBINE_TILE
    per_batch = seq // tm
    row = lambda w: pl.BlockSpec((tm, w), lambda i, dst: (i, 0))
    vec = pl.BlockSpec((1, d), lambda i, dst: (0, 0))
    grid_spec = pltpu.PrefetchScalarGridSpec(
        num_scalar_prefetch=1,
        grid=(t // tm,),
        in_specs=[pl.BlockSpec(memory_space=pl.ANY), row(LANES), row(d),
                  pl.BlockSpec((1, 1, d), lambda i, dst: (i // per_batch, 0, 0)), vec, vec],
        out_specs=row(d),
        scratch_shapes=[pltpu.VMEM((2, TOPK_IN_GROUP * tm * (d // LANES), LANES), F32),
                        pltpu.SemaphoreType.DMA((2,))],
    )
    return pl.pallas_call(
        _combine_kernel,
        grid_spec=grid_spec,
        out_shape=jax.ShapeDtypeStruct((t, d), F32),
        compiler_params=_cparams("arbitrary"),
    )(dst, y_sorted, route, x, gate.reshape(n_batch, 1, d), ln_g.reshape(1, d), ln_b.reshape(1, d))


def _sort_plan(route, seq):
    t = route.shape[0]
    tm = MOE_TILE
    eid = route[:, :TOPK_IN_GROUP].astype(jnp.int32).reshape(-1)
    onehot = (eid[:, None] == jnp.arange(N_EXPERTS)[None, :]).astype(jnp.int32)
    before = jnp.cumsum(onehot, axis=0) - onehot
    rank = jnp.sum(before * onehot, axis=1)
    counts = jnp.sum(onehot, axis=0)
    tiles = (counts + tm - 1) // tm
    tile_end = jnp.cumsum(tiles)
    tile_start = tile_end - tiles
    dst = tile_start[eid] * tm + rank
    n_tiles = (t * TOPK_IN_GROUP) // tm + N_EXPERTS
    tile_ids = jnp.arange(n_tiles)
    tile_expert = jnp.sum((tile_ids[:, None] >= tile_end[None, :]).astype(jnp.int32), axis=1)
    tile_expert = jnp.minimum(tile_expert, N_EXPERTS - 1)
    n_used = tile_end[-1:].astype(jnp.int32)
    tok = jnp.arange(t * TOPK_IN_GROUP, dtype=jnp.int32) // TOPK_IN_GROUP
    src_tok = jnp.zeros((n_tiles * tm,), jnp.int32).at[dst].set(tok)
    row_batch = (src_tok // seq).reshape(-1, 1)
    return dst.astype(jnp.int32), tile_expert.astype(jnp.int32), src_tok, n_used, row_batch


def _moe_layer(x, x_tm, scale, shift, gate, w_group, b_group, w_router, b_router, w1, w3, w2,
               ln_g, ln_b, seq):
    route = _router(x, scale, shift, w_group, b_group, w_router, b_router, seq)
    dst, tile_expert, src_tok, n_used, row_batch = _sort_plan(route, seq)
    y_sorted = _experts(x_tm, scale, shift, tile_expert, src_tok, n_used, row_batch, w1, w3, w2)
    return _combine_ln(dst, y_sorted, route, x, gate, ln_g, ln_b, seq)


def _swa_layer(x, scale, shift, gate, tables, w_qkv, b_qkv, sinks, w_o, ln_g, ln_b, seq):
    qw = Q_HEADS * HEAD_DIM
    kvw = KV_HEADS * HEAD_DIM
    tn = 256
    qkv = _mod_proj(x, scale, shift, w_qkv.astype(BF16), b_qkv, tables, seq,
                    rope_lo=qw // tn, rope_hi=(qw + kvw) // tn, tn=tn)
    o = _band_attention(qkv, 0, qw // kvw, qw // kvw + 1, tables, seq, SWA_WINDOW, sinks=sinks)
    return _out_proj_ln([o], w_o.astype(BF16), x, gate, ln_g, ln_b, seq)


def _nsa_layer(x, scale, shift, gate, tables, w_in, pe_k, pe_v, phi_k1, phi_k2, phi_v1, phi_v2,
               w_o, ln_g, ln_b, seq):
    t, d = x.shape
    n_batch = t // seq
    qw = Q_HEADS * HEAD_DIM
    kvw = KV_HEADS * HEAD_DIM
    tn = 256
    cols = [qw + i * kvw for i in range(7)]
    w_q, w_kc, w_vc, w_ks, w_vs, w_kw, w_vw, w_g = jnp.split(w_in, cols, axis=1)
    w_g = w_g.reshape(d, Q_HEADS, 3).transpose(0, 2, 1).reshape(d, 3 * Q_HEADS)
    w_g = jnp.pad(w_g, ((0, 0), (0, tn - 3 * Q_HEADS)))
    w_all = jnp.concatenate([w_q, w_ks, w_kw, w_vs, w_vw, w_kc, w_vc, w_g], axis=1).astype(BF16)
    n_cols = w_all.shape[1]
    proj = _mod_proj(x, scale, shift, w_all, jnp.zeros((n_cols,), F32), tables, seq,
                     rope_lo=qw // tn, rope_hi=(qw + 2 * kvw) // tn, tn=tn)
    kv_blk = qw // kvw
    k_s, v_s, k_c, v_c = (proj[:, qw + i * kvw: qw + (i + 1) * kvw] for i in (0, 2, 4, 5))
    gate_blk = (qw + 6 * kvw) // LANES

    kc = _compress(k_c, pe_k, phi_k1, phi_k2, seq)
    vc = _compress(v_c, pe_v, phi_v1, phi_v2, seq)
    o_cmp, selb = _cmp_attention(proj, kc, vc, gate_blk, seq, gate_col=0)

    heads = lambda a: a.reshape(n_batch, seq, KV_HEADS, HEAD_DIM).transpose(0, 2, 1, 3)
    onehot = (jnp.arange(seq)[:, None] // SEL_BLOCK == jnp.arange(HEAD_DIM)[None, :]).astype(BF16)
    onehot = jnp.broadcast_to(onehot, (n_batch, KV_HEADS, seq, HEAD_DIM))
    ks_h, vs_h = heads(k_s), heads(v_s)
    ka_even = jnp.concatenate([ks_h, onehot], axis=-1)
    ka_odd = jnp.concatenate([onehot, ks_h], axis=-1)
    vs2 = jnp.concatenate([vs_h, vs_h], axis=-1)
    o_sel = _sel_attention(proj, selb, ka_even, ka_odd, vs2, tables, gate_blk, seq, gate_col=Q_HEADS)
    o_win = _band_attention(proj, 0, kv_blk + 1, kv_blk + 3, tables, seq, NSA_WINDOW,
                            gate_blk=gate_blk, gate_col=2 * Q_HEADS)
    return _out_proj_ln([o_cmp, o_sel, o_win], w_o.astype(BF16), x, gate, ln_g, ln_b, seq)


def kernel(x, c, positions, w_ada, b_ada, swa_w_qkv, swa_b_qkv, swa_sinks, swa_w_o, nsa_w_in,
           nsa_pe_k, nsa_pe_v, nsa_phi_k1, nsa_phi_k2, nsa_phi_v1, nsa_phi_v2, nsa_w_o,
           moe_w_group, moe_b_group, moe_w_router, moe_b_router, moe_w1, moe_w3, moe_w2,
           ln_t_g, ln_t_b, ln_c_g, ln_c_b):
    n_batch, seq, d = x.shape
    depth = w_ada.shape[0]
    xt = x.reshape(n_batch * seq, d)
    mod = _adaln_mod(c, w_ada, b_ada)
    tables = _rope_tables(positions)
    for i in range(depth):
        sh_t, sc_t, g_t, sh_c, sc_c, g_c = (mod[i, :, k * d:(k + 1) * d] for k in range(6))
        j = i // 2
        if i % 2 == 0:
            xt, x_tm = _swa_layer(xt, sc_t, sh_t, g_t, tables, swa_w_qkv[j], swa_b_qkv[j], swa_sinks[j],
                            swa_w_o[j], ln_t_g[i], ln_t_b[i], seq)
        else:
            xt, x_tm = _nsa_layer(xt, sc_t, sh_t, g_t, tables, nsa_w_in[j], nsa_pe_k[j], nsa_pe_v[j],
                            nsa_phi_k1[j], nsa_phi_k2[j], nsa_phi_v1[j], nsa_phi_v2[j], nsa_w_o[j],
                            ln_t_g[i], ln_t_b[i], seq)
        xt = _moe_layer(xt, x_tm, sc_c, sh_c, g_c, moe_w_group[i], moe_b_group[i], moe_w_router[i],
                        moe_b_router[i], moe_w1[i], moe_w3[i], moe_w2[i], ln_c_g[i], ln_c_b[i], seq)
    return xt.reshape(n_batch, seq, d)
```

```python
import functools

import jax
import jax.numpy as jnp
from jax import lax
from jax.experimental import pallas as pl
from jax.experimental.pallas import tpu as pltpu

F32 = jnp.float32
BF16 = jnp.bfloat16

HEAD_DIM = 64
ROPE_DIM = HEAD_DIM // 4
ROPE_HALF = ROPE_DIM // 2
ROPE_THETA = 500000.0
Q_HEADS = 32
KV_HEADS = 4
GROUP = Q_HEADS // KV_HEADS
SWA_WINDOW = 128
NSA_WINDOW = 512
CMP_BLOCK = 32
CMP_STRIDE = 16
SEL_BLOCK = 64
SEL_TOPK = 16
FORCE_BONUS = 1e4
N_GROUPS = 4
EXPERTS_PER_GROUP = 4
N_EXPERTS = N_GROUPS * EXPERTS_PER_GROUP
TOPK_IN_GROUP = 2
DEPTH = 2
ALPHA = (2 * DEPTH) ** 0.25
LN_EPS = 1e-5
NEG_INF = -1e30
LOG2E = 1.4426950408889634

LANES = 128
ATTN_BLOCK = 128
VMEM_LIMIT = 56 * 1024 * 1024


def _cparams(*sem):
    return pltpu.CompilerParams(dimension_semantics=sem, vmem_limit_bytes=VMEM_LIMIT)


def _mod_kernel(cb_ref, w_ref, b_ref, o_ref, cs_ref, *, n_batch, tn):
    @pl.when((pl.program_id(0) == 0) & (pl.program_id(1) == 0))
    def _():
        c = cb_ref[...]
        cs_ref[...] = c * (1.0 / (1.0 + jnp.exp(-c)))

    for b in range(n_batch):
        cs = cs_ref[b]
        parts = []
        for g in range(tn // LANES):
            wg = w_ref[0, :, g * LANES:(g + 1) * LANES]
            parts.append(jnp.sum(wg * cs, axis=0, keepdims=True))
        o_ref[0, b:b + 1, :] = jnp.concatenate(parts, axis=-1) + b_ref[0]


def _adaln_mod(c, w_ada, b_ada):
    n_batch, d = c.shape
    depth, _, n = w_ada.shape
    tn = 512
    cb = jnp.broadcast_to(c[:, :, None], (n_batch, d, LANES))
    return pl.pallas_call(
        functools.partial(_mod_kernel, n_batch=n_batch, tn=tn),
        grid=(depth, n // tn),
        in_specs=[
            pl.BlockSpec((n_batch, d, LANES), lambda l, j: (0, 0, 0)),
            pl.BlockSpec((1, d, tn), lambda l, j: (l, 0, j)),
            pl.BlockSpec((1, 1, tn), lambda l, j: (l, 0, j)),
        ],
        out_specs=pl.BlockSpec((1, n_batch, tn), lambda l, j: (l, 0, j)),
        out_shape=jax.ShapeDtypeStruct((depth, n_batch, n), F32),
        scratch_shapes=[pltpu.VMEM((n_batch, d, LANES), F32)],
        compiler_params=_cparams("arbitrary", "arbitrary"),
    )(cb, w_ada, b_ada.reshape(depth, 1, n))


def _rope_table_kernel(pos_ref, inv_ref, c_ref, s1_ref, s2_ref):
    ang = pos_ref[...] * inv_ref[...]
    d = lax.broadcasted_iota(jnp.int32, ang.shape, 1) % HEAD_DIM
    cos = jnp.cos(ang)
    sin = jnp.sin(ang)
    c_ref[...] = jnp.where(d < ROPE_DIM, cos, 1.0)
    s1_ref[...] = jnp.where(d < ROPE_HALF, -sin, 0.0)
    s2_ref[...] = jnp.where((d >= ROPE_HALF) & (d < ROPE_DIM), sin, 0.0)


def _rope_tables(positions):
    t = positions.size
    tm = 1024
    inv_freq = ROPE_THETA ** (-jnp.arange(0, ROPE_DIM, 2, dtype=F32) / ROPE_DIM)
    lane = jnp.arange(LANES) % HEAD_DIM
    inv_lane = inv_freq[lane % ROPE_HALF].reshape(1, LANES)
    pos = positions.astype(F32).reshape(t, 1)
    spec = pl.BlockSpec((tm, LANES), lambda i: (i, 0))
    return pl.pallas_call(
        _rope_table_kernel,
        grid=(t // tm,),
        in_specs=[pl.BlockSpec((tm, 1), lambda i: (i, 0)),
                  pl.BlockSpec((1, LANES), lambda i: (0, 0))],
        out_specs=[spec, spec, spec],
        out_shape=[jax.ShapeDtypeStruct((t, LANES), F32)] * 3,
        compiler_params=_cparams("arbitrary"),
    )(pos, inv_lane)


def _apply_rope(x, c, s1, s2):
    reps = x.shape[-1] // LANES
    if reps > 1:
        c = jnp.concatenate([c] * reps, axis=-1)
        s1 = jnp.concatenate([s1] * reps, axis=-1)
        s2 = jnp.concatenate([s2] * reps, axis=-1)
    n = x.shape[-1]
    up = pltpu.roll(x, n - ROPE_HALF, 1)
    down = pltpu.roll(x, ROPE_HALF, 1)
    return x * c + up * s1 + down * s2


def _proj_kernel(x_ref, sc_ref, sh_ref, w_ref, b_ref, c_ref, s1_ref, s2_ref, o_ref, h_ref,
                 *, rope_lo, rope_hi):
    j = pl.program_id(1)

    @pl.when(j == 0)
    def _():
        h_ref[...] = (x_ref[...] * (1.0 + sc_ref[0]) + sh_ref[0]).astype(BF16)

    acc = jnp.dot(h_ref[...], w_ref[...], preferred_element_type=F32) + b_ref[...]
    if rope_hi > rope_lo:
        is_rope = (j >= rope_lo) & (j < rope_hi)

        @pl.when(is_rope)
        def _():
            o_ref[...] = _apply_rope(acc, c_ref[...], s1_ref[...], s2_ref[...]).astype(o_ref.dtype)

        @pl.when(jnp.logical_not(is_rope))
        def _():
            o_ref[...] = acc.astype(o_ref.dtype)
    else:
        o_ref[...] = acc.astype(o_ref.dtype)


def _mod_proj(x, scale, shift, w, bias, tables, seq, *, rope_lo=0, rope_hi=0, tn=256, tm=1024,
              out_dtype=BF16):
    t, d = x.shape
    n = w.shape[1]
    n_batch = scale.shape[0]
    tm = min(tm, seq)
    per_batch = seq // tm
    c_tab, s1_tab, s2_tab = tables
    vec = pl.BlockSpec((1, 1, d), lambda i, j: (i // per_batch, 0, 0))
    tab = pl.BlockSpec((tm, LANES), lambda i, j: (i, 0))
    return pl.pallas_call(
        functools.partial(_proj_kernel, rope_lo=rope_lo, rope_hi=rope_hi),
        grid=(t // tm, n // tn),
        in_specs=[
            pl.BlockSpec((tm, d), lambda i, j: (i, 0)),
            vec, vec,
            pl.BlockSpec((d, tn), lambda i, j: (0, j)),
            pl.BlockSpec((1, tn), lambda i, j: (0, j)),
            tab, tab, tab,
        ],
        out_specs=pl.BlockSpec((tm, tn), lambda i, j: (i, j)),
        out_shape=jax.ShapeDtypeStruct((t, n), out_dtype),
        scratch_shapes=[pltpu.VMEM((tm, d), BF16)],
        compiler_params=_cparams("arbitrary", "arbitrary"),
    )(x, scale.reshape(n_batch, 1, d), shift.reshape(n_batch, 1, d), w, bias.reshape(1, n),
      c_tab, s1_tab, s2_tab)


def _scaled_q(q_ref, tables=None):
    q = q_ref[...].astype(F32)
    if tables is not None:
        q = _apply_rope(q, *(t[...] for t in tables))
    return q * (HEAD_DIM ** -0.5 * LOG2E)


def _split_even_odd(q):
    even = (lax.broadcasted_iota(jnp.int32, q.shape, 1) % LANES) < HEAD_DIM
    return jnp.where(even, q, 0.0).astype(BF16), jnp.where(even, 0.0, q).astype(BF16)


def _stack_heads(q_even, q_odd, first_head):
    parts = []
    for g in range(GROUP):
        head = first_head + g
        src = q_even if head % 2 == 0 else q_odd
        parts.append(src[:, (head // 2) * LANES:(head // 2 + 1) * LANES])
    return jnp.concatenate(parts, axis=0)


def _twice(x):
    return jnp.concatenate([x, x], axis=-1)


def _merge_pairs(o, rows, scale):
    even = lax.broadcasted_iota(jnp.int32, (rows, LANES), 1) < HEAD_DIM
    out = []
    for g in range(0, GROUP, 2):
        a = o[g * rows:(g + 1) * rows] * scale[g]
        b = o[(g + 1) * rows:(g + 2) * rows] * scale[g + 1]
        out.append(jnp.where(even, a, b))
    return jnp.concatenate(out, axis=-1)


def _sigmoid(x):
    return 1.0 / (1.0 + jnp.exp(-x))


def _band_attn_kernel(*refs, window, use_sinks, use_gate, gate_col):
    it = iter(refs)
    q_ref, k_ref, v_ref, c_ref, s1_ref, s2_ref = (next(it) for _ in range(6))
    sink_ref = next(it) if use_sinks else None
    gate_ref = next(it) if use_gate else None
    o_ref = next(it)

    n = pl.program_id(1)
    rows = ATTN_BLOCK
    halo = -(-window // rows) * rows
    span = rows + halo
    start = jnp.maximum(n * rows - halo, 0)
    start = pl.multiple_of(start, rows)
    q_even, q_odd = _split_even_odd(_scaled_q(q_ref, (c_ref, s1_ref, s2_ref)))
    k = k_ref[pl.ds(start, span), :]
    v = v_ref[pl.ds(start, span), :]
    qpos = n * rows + lax.broadcasted_iota(jnp.int32, (rows, span), 0)
    kpos = start + lax.broadcasted_iota(jnp.int32, (rows, span), 1)
    rel = qpos - kpos
    mask = (rel >= 0) & (rel < window)
    if use_gate:
        gate = _sigmoid(gate_ref[...].astype(F32))

    for j in range(KV_HEADS):
        k2 = _twice(k[:, j * HEAD_DIM:(j + 1) * HEAD_DIM])
        v2 = _twice(v[:, j * HEAD_DIM:(j + 1) * HEAD_DIM])
        q8 = _stack_heads(q_even, q_odd, j * GROUP)
        s = lax.dot_general(q8, k2, (((1,), (1,)), ((), ())), preferred_element_type=F32)
        ps, scale = [], []
        for g in range(GROUP):
            sg = jnp.where(mask, s[g * rows:(g + 1) * rows], NEG_INF)
            m = jnp.max(sg, axis=-1, keepdims=True)
            if use_sinks:
                sink = sink_ref[j * GROUP + g] * LOG2E
                m = jnp.maximum(m, sink)
            e = jnp.exp2(sg - m)
            den = jnp.sum(e, axis=-1, keepdims=True)
            if use_sinks:
                den = den + jnp.exp2(sink - m)
            inv = 1.0 / den
            if use_gate:
                col = gate_col + j * GROUP + g
                inv = inv * gate[:, col:col + 1]
            ps.append(e.astype(BF16))
            scale.append(inv)
        o = jnp.dot(jnp.concatenate(ps, axis=0), v2, preferred_element_type=F32)
        o_ref[:, j * GROUP * HEAD_DIM:(j + 1) * GROUP * HEAD_DIM] = (
            _merge_pairs(o, rows, scale).astype(o_ref.dtype))


def _band_attention(proj, q_blk, k_blk, v_blk, tables, seq, window, sinks=None, gate_blk=None,
                    gate_col=0):
    t = proj.shape[0]
    n_batch = t // seq
    nb = seq // ATTN_BLOCK
    qw = Q_HEADS * HEAD_DIM
    kvw = KV_HEADS * HEAD_DIM
    row = lambda w, c: pl.BlockSpec((ATTN_BLOCK, w), lambda b, i: (b * nb + i, c))
    res = lambda c: pl.BlockSpec((seq, kvw), lambda b, i: (b, c))
    in_specs = [row(qw, q_blk), res(k_blk), res(v_blk), row(LANES, 0), row(LANES, 0), row(LANES, 0)]
    args = [proj, proj, proj, *tables]
    if sinks is not None:
        in_specs.append(pl.BlockSpec(memory_space=pltpu.SMEM))
        args.append(sinks)
    if gate_blk is not None:
        in_specs.append(row(LANES, gate_blk))
        args.append(proj)
    return pl.pallas_call(
        functools.partial(_band_attn_kernel, window=window, use_sinks=sinks is not None,
                          use_gate=gate_blk is not None, gate_col=gate_col),
        grid=(n_batch, nb),
        in_specs=in_specs,
        out_specs=row(qw, 0),
        out_shape=jax.ShapeDtypeStruct((t, qw), BF16),
        compiler_params=_cparams("arbitrary", "arbitrary"),
    )(*args)


def _layer_norm(y, g, b):
    mu = jnp.mean(y, axis=-1, keepdims=True)
    yc = y - mu
    var = jnp.mean(yc * yc, axis=-1, keepdims=True)
    return yc * lax.rsqrt(var + LN_EPS) * g + b


def _out_proj_kernel(*refs, n_parts):
    o_parts = refs[:n_parts]
    w_ref, x_ref, gate_ref, g_ref, b_ref, out_ref, out_tm_ref = refs[n_parts:]
    if n_parts == 1:
        o = o_parts[0][...]
    else:
        acc = o_parts[0][...].astype(F32)
        for r in o_parts[1:]:
            acc = acc + r[...].astype(F32)
        o = acc.astype(BF16)
    mix = jnp.dot(o, w_ref[...], preferred_element_type=F32)
    y = ALPHA * x_ref[...] + gate_ref[0] * mix
    res = _layer_norm(y, g_ref[...], b_ref[...])
    out_ref[...] = res
    _store_token_major(out_tm_ref, res)


def _out_proj_ln(o_parts, w_o, x, gate, ln_g, ln_b, seq, tm=256):
    t, d = x.shape
    n = d // LANES
    n_batch = gate.shape[0]
    per_batch = seq // tm
    k = w_o.shape[0]
    row_o = pl.BlockSpec((tm, k), lambda i: (i, 0))
    row_x = pl.BlockSpec((tm, d), lambda i: (i, 0))
    vec = pl.BlockSpec((1, d), lambda i: (0, 0))
    return pl.pallas_call(
        functools.partial(_out_proj_kernel, n_parts=len(o_parts)),
        grid=(t // tm,),
        in_specs=[row_o] * len(o_parts) + [
            pl.BlockSpec((k, d), lambda i: (0, 0)),
            row_x,
            pl.BlockSpec((1, 1, d), lambda i: (i // per_batch, 0, 0)),
            vec, vec],
        out_specs=[row_x, pl.BlockSpec((tm * n, LANES), lambda i: (i, 0))],
        out_shape=[jax.ShapeDtypeStruct((t, d), F32), jax.ShapeDtypeStruct((t * n, LANES), F32)],
        compiler_params=_cparams("arbitrary"),
    )(*o_parts, w_o, x, gate.reshape(n_batch, 1, d), ln_g.reshape(1, d), ln_b.reshape(1, d))


def _compress_kernel(x_ref, pe_ref, w1_ref, w1a_ref, w1b_ref, w2_ref, o_ref):
    x = x_ref[0]
    a = jnp.dot(x, w1a_ref[...], preferred_element_type=F32)
    b = jnp.dot(x, w1b_ref[...], preferred_element_type=F32)
    nc = a.shape[0]
    b_next = pltpu.roll(b, nc - 1, 0)
    pe_term = jnp.dot(pe_ref[...], w1_ref[...], preferred_element_type=F32)[0:1]
    hid = a + b_next + jnp.concatenate([pe_term] * KV_HEADS, axis=-1)
    hid = hid * (1.0 / (1.0 + jnp.exp(-hid)))
    o_ref[0] = jnp.dot(hid.astype(BF16), w2_ref[...], preferred_element_type=F32).astype(o_ref.dtype)


def _block_diag_heads(w):
    p, d, n = w.shape
    eye = jnp.eye(KV_HEADS, dtype=w.dtype)
    big = w[:, None, :, None, :] * eye[None, :, None, :, None]
    return big.reshape(p * KV_HEADS * d, KV_HEADS * n)


def _compress(xc, pe, w1, w2, seq):
    t, kvw = xc.shape
    n_batch = t // seq
    nch = seq // CMP_STRIDE
    hidden = w1.shape[1]
    x = xc.reshape(n_batch, nch, CMP_STRIDE * kvw)
    w1r = w1.reshape(CMP_BLOCK, HEAD_DIM, hidden)
    w1a = _block_diag_heads(w1r[:CMP_STRIDE]).astype(BF16)
    w1b = _block_diag_heads(w1r[CMP_STRIDE:]).astype(BF16)
    eye = jnp.eye(KV_HEADS, dtype=w2.dtype)
    w2d = (w2[None, :, None, :] * eye[:, None, :, None]).reshape(KV_HEADS * hidden, kvw).astype(BF16)
    pe8 = jnp.broadcast_to(pe.reshape(1, CMP_BLOCK * HEAD_DIM), (8, CMP_BLOCK * HEAD_DIM)).astype(BF16)
    full = lambda a: pl.BlockSpec(a.shape, lambda b: (0,) * a.ndim)
    w1b16 = w1.astype(BF16)
    return pl.pallas_call(
        _compress_kernel,
        grid=(n_batch,),
        in_specs=[pl.BlockSpec((1, nch, CMP_STRIDE * kvw), lambda b: (b, 0, 0)),
                  full(pe8), full(w1b16), full(w1a), full(w1b), full(w2d)],
        out_specs=pl.BlockSpec((1, nch, kvw), lambda b: (b, 0, 0)),
        out_shape=jax.ShapeDtypeStruct((n_batch, nch, kvw), BF16),
        compiler_params=_cparams("arbitrary"),
    )(x, pe8, w1b16, w1a, w1b, w2d)


def _topk_bias(score, top_k):
    nblk, rows = score.shape
    sub = lax.broadcasted_iota(jnp.int32, (8, rows), 0)
    chunks = [score[c * 8:(c + 1) * 8] for c in range(nblk // 8)]
    counts = [jnp.zeros((8, rows), F32) for _ in chunks]
    for sp in range(nblk):
        row = score[sp:sp + 1]
        for c, chunk in enumerate(chunks):
            ge = jnp.where(row >= chunk, 1.0, 0.0)
            gt = jnp.where(row > chunk, 1.0, 0.0)
            if sp < c * 8:
                beats = ge
            elif sp >= (c + 1) * 8:
                beats = gt
            else:
                beats = jnp.where(sub > sp - c * 8, ge, gt)
            counts[c] = counts[c] + beats
    return jnp.concatenate([jnp.where(cnt < top_k, 0.0, NEG_INF) for cnt in counts], axis=0)


def _cmp_attn_kernel(q_ref, kc_ref, vc_ref, ovt_ref, gate_ref, o_ref, sel_ref, *, gate_col, top_k):
    n = pl.program_id(1)
    rows = ATTN_BLOCK
    nc = kc_ref.shape[1]
    nblk = ovt_ref.shape[0]
    q_even, q_odd = _split_even_odd(_scaled_q(q_ref))
    kc = kc_ref[0]
    vc = vc_ref[0]
    t = n * rows + lax.broadcasted_iota(jnp.int32, (rows, nc), 0)
    cmp_end = lax.broadcasted_iota(jnp.int32, (rows, nc), 1) * CMP_STRIDE + (CMP_BLOCK - 1)
    valid = cmp_end <= t
    has_valid = n * rows + lax.broadcasted_iota(jnp.int32, (rows, 1), 0) >= CMP_BLOCK - 1
    gate = _sigmoid(gate_ref[...].astype(F32))

    ts = n * rows + lax.broadcasted_iota(jnp.int32, (nblk, rows), 1)
    blk = lax.broadcasted_iota(jnp.int32, (nblk, rows), 0)
    cur = lax.shift_right_arithmetic(ts, SEL_BLOCK.bit_length() - 1)
    causal = blk * SEL_BLOCK <= ts
    forced = (blk == 0) | (blk == cur) | (blk == cur - 1)
    bonus = jnp.where(forced, FORCE_BONUS, 0.0)

    for j in range(KV_HEADS):
        k2 = _twice(kc[:, j * HEAD_DIM:(j + 1) * HEAD_DIM])
        v2 = _twice(vc[:, j * HEAD_DIM:(j + 1) * HEAD_DIM])
        q8 = _stack_heads(q_even, q_odd, j * GROUP)
        s = lax.dot_general(q8, k2, (((1,), (1,)), ((), ())), preferred_element_type=F32)
        ps, scale, psum = [], [], None
        for g in range(GROUP):
            sg = jnp.where(valid, s[g * rows:(g + 1) * rows], NEG_INF)
            m = jnp.max(sg, axis=-1, keepdims=True)
            e = jnp.exp2(sg - m)
            den = jnp.sum(e, axis=-1, keepdims=True)
            inv = jnp.where(has_valid, 1.0 / den, 0.0)
            pn = e * inv
            psum = pn if psum is None else psum + pn
            col = gate_col + j * GROUP + g
            ps.append(e.astype(BF16))
            scale.append(inv * gate[:, col:col + 1])
        o = jnp.dot(jnp.concatenate(ps, axis=0), v2, preferred_element_type=F32)
        o_ref[:, j * GROUP * HEAD_DIM:(j + 1) * GROUP * HEAD_DIM] = (
            _merge_pairs(o, rows, scale).astype(o_ref.dtype))

        imp_t = lax.dot_general(ovt_ref[...], psum, (((1,), (1,)), ((), ())),
                                preferred_element_type=F32,
                                precision=lax.Precision.HIGHEST)
        score = jnp.where(causal, imp_t + bonus, NEG_INF)
        bias_t = _topk_bias(score, top_k)
        sel_ref[0, j] = jnp.concatenate([bias_t, bias_t], axis=0).T.astype(sel_ref.dtype)


def _cmp_attention(proj, kc, vc, gate_blk, seq, gate_col):
    t = proj.shape[0]
    n_batch = t // seq
    nb = seq // ATTN_BLOCK
    nc = kc.shape[1]
    nsel = seq // SEL_BLOCK
    assert nsel <= HEAD_DIM
    qw = Q_HEADS * HEAD_DIM
    top_k = min(SEL_TOPK, nsel)
    c0 = jnp.arange(nc)[None, :] * CMP_STRIDE
    s0 = jnp.arange(HEAD_DIM)[:, None] * SEL_BLOCK
    ov = jnp.clip(jnp.minimum(c0 + CMP_BLOCK, s0 + SEL_BLOCK) - jnp.maximum(c0, s0), 0)
    overlap_t = ov.astype(F32) / CMP_BLOCK
    row = lambda w, c: pl.BlockSpec((ATTN_BLOCK, w), lambda b, i: (b * nb + i, c))
    res = pl.BlockSpec((1, nc, kc.shape[2]), lambda b, i: (b, 0, 0))
    return pl.pallas_call(
        functools.partial(_cmp_attn_kernel, gate_col=gate_col, top_k=top_k),
        grid=(n_batch, nb),
        in_specs=[row(qw, 0), res, res,
                  pl.BlockSpec((HEAD_DIM, nc), lambda b, i: (0, 0)), row(LANES, gate_blk)],
        out_specs=[row(qw, 0),
                   pl.BlockSpec((1, KV_HEADS, ATTN_BLOCK, LANES), lambda b, i: (b, 0, i, 0))],
        out_shape=[jax.ShapeDtypeStruct((t, qw), BF16),
                   jax.ShapeDtypeStruct((n_batch, KV_HEADS, seq, LANES), BF16)],
        compiler_params=_cparams("arbitrary", "arbitrary"),
    )(proj, kc, vc, overlap_t, proj)


SEL_TILE = 512


def _sel_attn_kernel(q_ref, sel_ref, kae_ref, kao_ref, v_ref, c_ref, s1_ref, s2_ref, gate_ref, o_ref,
                     m_ref, l_ref, acc_ref, *, gate_col):
    j = pl.program_id(1)
    n = pl.program_id(2)
    rows = ATTN_BLOCK
    tk = SEL_TILE
    pairs = GROUP // 2
    q = _scaled_q(q_ref, (c_ref, s1_ref, s2_ref))
    bias2 = sel_ref[0, 0].astype(F32)
    low = lax.broadcasted_iota(jnp.int32, (rows, LANES), 1) < HEAD_DIM
    lhs = []
    for parity in range(2):
        parts = []
        for i in range(pairs):
            slab = q[:, i * LANES:(i + 1) * LANES]
            parts.append(jnp.where(low, slab, bias2) if parity == 0 else jnp.where(low, bias2, slab))
        lhs.append(jnp.concatenate(parts, axis=0).astype(BF16))
    ka_refs = (kae_ref, kao_ref)

    m_ref[...] = jnp.full(m_ref.shape, NEG_INF, F32)
    l_ref[...] = jnp.zeros(l_ref.shape, F32)
    acc_ref[...] = jnp.zeros(acc_ref.shape, F32)

    def tile(kt, mask_bias):
        off = pl.multiple_of(kt * tk, tk)
        v2 = v_ref[0, 0, pl.ds(off, tk), :]
        scores = []
        for parity in range(2):
            ka = ka_refs[parity][0, 0, pl.ds(off, tk), :]
            s = lax.dot_general(lhs[parity], ka, (((1,), (1,)), ((), ())),
                                preferred_element_type=F32)
            scores.append(s if mask_bias is None else s + mask_bias)
        probs, alphas = [], []
        for parity in range(2):
            s = scores[parity]
            chunks = [s[:, c * LANES:(c + 1) * LANES] for c in range(tk // LANES)]
            rmax = chunks[0]
            for ch in chunks[1:]:
                rmax = jnp.maximum(rmax, ch)
            m_old = m_ref[parity]
            m_new = jnp.maximum(m_old, jnp.max(rmax, axis=-1, keepdims=True))
            alpha = jnp.exp2(m_old - m_new)
            ps = [jnp.exp2(ch - m_new) for ch in chunks]
            lsum = ps[0]
            for x in ps[1:]:
                lsum = lsum + x
            l_ref[parity] = l_ref[parity] * alpha + lsum
            m_ref[parity] = m_new
            probs.append(jnp.concatenate([x.astype(BF16) for x in ps], axis=-1))
            alphas.append(alpha)
        for parity in range(2):
            acc_ref[parity] = acc_ref[parity] * alphas[parity] + jnp.dot(
                probs[parity], v2, preferred_element_type=F32)

    def body(kt, carry):
        tile(kt, None)
        return carry

    n_full = (n * rows) // tk
    lax.fori_loop(0, n_full, body, 0)
    qpos = n * rows + lax.broadcasted_iota(jnp.int32, (rows, tk), 0)
    kpos = n_full * tk + lax.broadcasted_iota(jnp.int32, (rows, tk), 1)
    causal = jnp.where(kpos <= qpos, 0.0, NEG_INF)
    tile(n_full, jnp.concatenate([causal] * pairs, axis=0))

    gate = _sigmoid(gate_ref[...].astype(F32))
    lane = lax.broadcasted_iota(jnp.int32, gate.shape, 1)
    outs = []
    for i in range(pairs):
        scaled = []
        for parity in range(2):
            col = gate_col + j * GROUP + 2 * i + parity
            gcol = jnp.sum(jnp.where(lane == col, gate, 0.0), axis=-1, keepdims=True)
            l = jnp.sum(l_ref[parity, i * rows:(i + 1) * rows], axis=-1, keepdims=True)
            scaled.append(acc_ref[parity, i * rows:(i + 1) * rows] * (gcol / l))
        outs.append(jnp.where(low, scaled[0], scaled[1]))
    o_ref[...] = jnp.concatenate(outs, axis=-1).astype(o_ref.dtype)


def _sel_attention(proj, selb, ka_even, ka_odd, v2, tables, gate_blk, seq, gate_col):
    t = proj.shape[0]
    n_batch = t // seq
    nb = seq // ATTN_BLOCK
    gw = GROUP * HEAD_DIM
    pairs = GROUP // 2
    row = lambda c: pl.BlockSpec((ATTN_BLOCK, LANES), lambda b, j, i: (b * nb + i, c))
    res = pl.BlockSpec((1, 1, seq, LANES), lambda b, j, i: (b, j, 0, 0))
    acc = pltpu.VMEM((2, pairs * ATTN_BLOCK, LANES), F32)
    return pl.pallas_call(
        functools.partial(_sel_attn_kernel, gate_col=gate_col),
        grid=(n_batch, KV_HEADS, nb),
        in_specs=[
            pl.BlockSpec((ATTN_BLOCK, gw), lambda b, j, i: (b * nb + i, j)),
            pl.BlockSpec((1, 1, ATTN_BLOCK, LANES), lambda b, j, i: (b, j, i, 0)),
            res, res, res, row(0), row(0), row(0), row(gate_blk)],
        out_specs=pl.BlockSpec((ATTN_BLOCK, gw), lambda b, j, i: (b * nb + i, j)),
        out_shape=jax.ShapeDtypeStruct((t, Q_HEADS * HEAD_DIM), BF16),
        scratch_shapes=[acc, acc, acc],
        compiler_params=_cparams("arbitrary", "arbitrary", "arbitrary"),
    )(proj, selb, ka_even, ka_odd, v2, *tables, proj)


def _router_kernel(x_ref, sc_ref, sh_ref, w_ref, b_ref, o_ref):
    h = x_ref[...] * (1.0 + sc_ref[0]) + sh_ref[0]
    logits = jnp.dot(h, w_ref[...], preferred_element_type=F32,
                     precision=lax.Precision.HIGHEST) + b_ref[...]
    lane = lax.broadcasted_iota(jnp.int32, logits.shape, 1)
    big = 1 << 20

    gmask = lane < N_GROUPS
    lg = jnp.where(gmask, logits, -jnp.inf)
    eg = jnp.exp(lg - jnp.max(lg, axis=-1, keepdims=True))
    pg = eg / jnp.sum(eg, axis=-1, keepdims=True)
    g_prob = jnp.max(pg, axis=-1, keepdims=True)
    g_idx = jnp.min(jnp.where((pg == g_prob) & gmask, lane, big), axis=-1, keepdims=True)

    lo = N_GROUPS + g_idx * EXPERTS_PER_GROUP
    emask = (lane >= lo) & (lane < lo + EXPERTS_PER_GROUP)
    le = jnp.where(emask, logits, -jnp.inf)
    ee = jnp.exp(le - jnp.max(le, axis=-1, keepdims=True))
    pe = jnp.where(emask, ee / jnp.sum(ee, axis=-1, keepdims=True), -1.0)
    p1 = jnp.max(pe, axis=-1, keepdims=True)
    i1 = jnp.min(jnp.where(pe == p1, lane, big), axis=-1, keepdims=True)
    pe2 = jnp.where(lane == i1, -1.0, pe)
    p2 = jnp.max(pe2, axis=-1, keepdims=True)
    i2 = jnp.min(jnp.where(pe2 == p2, lane, big), axis=-1, keepdims=True)
    tot = p1 + p2
    w1 = g_prob * (p1 / tot)
    w2 = g_prob * (p2 / tot)
    e1 = (i1 - N_GROUPS).astype(F32)
    e2 = (i2 - N_GROUPS).astype(F32)
    o_ref[...] = jnp.where(lane == 0, e1, jnp.where(lane == 1, e2,
                           jnp.where(lane == 2, w1, jnp.where(lane == 3, w2, 0.0))))


def _router(x, scale, shift, w_group, b_group, w_router, b_router, seq, tm=512):
    t, d = x.shape
    n_batch = scale.shape[0]
    per_batch = seq // tm
    w = jnp.zeros((d, LANES), F32).at[:, :N_GROUPS].set(w_group)
    w = w.at[:, N_GROUPS:N_GROUPS + N_EXPERTS].set(w_router)
    b = jnp.zeros((1, LANES), F32).at[0, :N_GROUPS].set(b_group)
    b = b.at[0, N_GROUPS:N_GROUPS + N_EXPERTS].set(b_router)
    vec = pl.BlockSpec((1, 1, d), lambda i: (i // per_batch, 0, 0))
    return pl.pallas_call(
        _router_kernel,
        grid=(t // tm,),
        in_specs=[pl.BlockSpec((tm, d), lambda i: (i, 0)), vec, vec,
                  pl.BlockSpec((d, LANES), lambda i: (0, 0)),
                  pl.BlockSpec((1, LANES), lambda i: (0, 0))],
        out_specs=pl.BlockSpec((tm, LANES), lambda i: (i, 0)),
        out_shape=jax.ShapeDtypeStruct((t, LANES), F32),
        compiler_params=_cparams("arbitrary"),
    )(x, scale.reshape(n_batch, 1, d), shift.reshape(n_batch, 1, d), w, b)


MOE_TILE = 256


def _store_token_major(ref, val):
    rows, n = val.shape[0], val.shape[1] // LANES
    for a in range(n):
        ref[pl.ds(a, rows, stride=n), :] = val[:, a * LANES:(a + 1) * LANES]


def _load_token_major(ref, rows, n):
    return jnp.concatenate([ref[pl.ds(a, rows, stride=n), :] for a in range(n)], axis=-1)


def _expert_kernel(te_ref, src_ref, nt_ref, x_hbm, rowb_ref, sc_ref, sh_ref, w1_ref, w3_ref, w2_ref,
                   o_ref, xbuf, w1b, w3b, w2b, sem, *, n_batch):
    i = pl.program_id(0)
    n_used = nt_ref[0]
    tm = MOE_TILE
    n = sc_ref.shape[-1] // LANES

    def row_copy(tile_idx, slot, r):
        tok = src_ref[tile_idx * tm + r]
        return pltpu.make_async_copy(x_hbm.at[pl.ds(pl.multiple_of(tok * n, n), n), :],
                                     xbuf.at[slot, pl.ds(pl.multiple_of(r * n, n), n), :],
                                     sem.at[slot])

    def start_tile(tile_idx, slot):
        def body(r, c):
            row_copy(tile_idx, slot, r).start()
            return c
        lax.fori_loop(0, tm, body, 0, unroll=8)

    def wait_tile(slot):
        pltpu.make_async_copy(x_hbm.at[pl.ds(0, tm * n), :], xbuf.at[slot], sem.at[slot]).wait()

    @pl.when((i == 0) & (n_used > 0))
    def _():
        start_tile(0, 0)

    @pl.when(i + 1 < n_used)
    def _():
        start_tile(i + 1, (i + 1) % 2)

    new_expert = (i == 0) | (te_ref[i] != te_ref[jnp.maximum(i - 1, 0)])

    @pl.when((i < n_used) & new_expert)
    def _():
        w1b[...] = w1_ref[0, 0].astype(BF16)
        w3b[...] = w3_ref[0, 0].astype(BF16)
        w2b[...] = w2_ref[0, 0].astype(BF16)

    @pl.when(i < n_used)
    def _():
        slot = i % 2
        wait_tile(slot)
        x = _load_token_major(xbuf.at[slot], tm, n)
        rowb = rowb_ref[...]
        h = x * (1.0 + sc_ref[0]) + sh_ref[0]
        for b in range(1, n_batch):
            h = jnp.where(rowb == b, x * (1.0 + sc_ref[b]) + sh_ref[b], h)
        hb = h.astype(BF16)
        a = jnp.dot(hb, w1b[...], preferred_element_type=F32)
        g = jnp.dot(hb, w3b[...], preferred_element_type=F32)
        he = (a * _sigmoid(a) * g).astype(BF16)
        _store_token_major(o_ref, jnp.dot(he, w2b[...], preferred_element_type=F32))

    @pl.when(i >= n_used)
    def _():
        o_ref[...] = jnp.zeros(o_ref.shape, o_ref.dtype)


def _experts(x_tm, scale, shift, tile_expert, src_tok, n_used, row_batch, w1, w3, w2, layer):
    n_batch, d = scale.shape
    n = d // LANES
    n_tiles = tile_expert.shape[0]
    de = w1.shape[3]
    tm = MOE_TILE
    grid_spec = pltpu.PrefetchScalarGridSpec(
        num_scalar_prefetch=3,
        grid=(n_tiles,),
        in_specs=[
            pl.BlockSpec(memory_space=pl.ANY),
            pl.BlockSpec((tm, 1), lambda i, te, src, nt: (i, 0)),
            pl.BlockSpec((n_batch, 1, d), lambda i, te, src, nt: (0, 0, 0)),
            pl.BlockSpec((n_batch, 1, d), lambda i, te, src, nt: (0, 0, 0)),
            pl.BlockSpec((1, 1, d, de), lambda i, te, src, nt: (layer, te[i], 0, 0)),
            pl.BlockSpec((1, 1, d, de), lambda i, te, src, nt: (layer, te[i], 0, 0)),
            pl.BlockSpec((1, 1, de, d), lambda i, te, src, nt: (layer, te[i], 0, 0)),
        ],
        out_specs=pl.BlockSpec((tm * n, LANES), lambda i, te, src, nt: (i, 0)),
        scratch_shapes=[pltpu.VMEM((2, tm * n, LANES), F32),
                        pltpu.VMEM((d, de), BF16), pltpu.VMEM((d, de), BF16),
                        pltpu.VMEM((de, d), BF16), pltpu.SemaphoreType.DMA((2,))],
    )
    return pl.pallas_call(
        functools.partial(_expert_kernel, n_batch=n_batch),
        grid_spec=grid_spec,
        out_shape=jax.ShapeDtypeStruct((n_tiles * tm * n, LANES), F32),
        compiler_params=_cparams("arbitrary"),
    )(tile_expert, src_tok, n_used, x_tm, row_batch, scale.reshape(n_batch, 1, d),
      shift.reshape(n_batch, 1, d), w1, w3, w2)


COMBINE_TILE = 256


def _combine_kernel(dst_ref, y_hbm, route_ref, x_ref, gate_ref, g_ref, b_ref, out_ref, ybuf, sem):
    i = pl.program_id(0)
    n_steps = pl.num_programs(0)
    tm = COMBINE_TILE

    n = x_ref.shape[-1] // LANES

    def row_copy(step, slot, k, r):
        pos = dst_ref[(step * tm + r) * TOPK_IN_GROUP + k]
        return pltpu.make_async_copy(
            y_hbm.at[pl.ds(pl.multiple_of(pos * n, n), n), :],
            ybuf.at[slot, pl.ds(pl.multiple_of((k * tm + r) * n, n), n), :], sem.at[slot])

    def start_step(step, slot):
        def body(r, c):
            for k in range(TOPK_IN_GROUP):
                row_copy(step, slot, k, r).start()
            return c
        lax.fori_loop(0, tm, body, 0, unroll=4)

    def wait_step(slot):
        pltpu.make_async_copy(y_hbm.at[pl.ds(0, TOPK_IN_GROUP * tm * n), :], ybuf.at[slot],
                              sem.at[slot]).wait()

    @pl.when(i == 0)
    def _():
        start_step(0, 0)

    @pl.when(i + 1 < n_steps)
    def _():
        start_step(i + 1, (i + 1) % 2)

    slot = i % 2
    wait_step(slot)
    route = route_ref[...]
    lane = lax.broadcasted_iota(jnp.int32, route.shape, 1)
    mix = None
    for k in range(TOPK_IN_GROUP):
        wk = jnp.sum(jnp.where(lane == TOPK_IN_GROUP + k, route, 0.0), axis=-1, keepdims=True)
        term = wk * _load_token_major(ybuf.at[slot, pl.ds(k * tm * n, tm * n), :], tm, n)
        mix = term if mix is None else mix + term
    y = ALPHA * x_ref[...] + gate_ref[0] * mix
    out_ref[...] = _layer_norm(y, g_ref[...], b_ref[...])


def _combine_ln(dst, y_sorted, route, x, gate, ln_g, ln_b, seq):
    t, d = x.shape
    n_batch = gate.shape[0]
    tm = COMBINE_TILE
    per_batch = seq // tm
    row = lambda w: pl.BlockSpec((tm, w), lambda i, dst: (i, 0))
    vec = pl.BlockSpec((1, d), lambda i, dst: (0, 0))
    grid_spec = pltpu.PrefetchScalarGridSpec(
        num_scalar_prefetch=1,
        grid=(t // tm,),
        in_specs=[pl.BlockSpec(memory_space=pl.ANY), row(LANES), row(d),
                  pl.BlockSpec((1, 1, d), lambda i, dst: (i // per_batch, 0, 0)), vec, vec],
        out_specs=row(d),
        scratch_shapes=[pltpu.VMEM((2, TOPK_IN_GROUP * tm * (d // LANES), LANES), F32),
                        pltpu.SemaphoreType.DMA((2,))],
    )
    return pl.pallas_call(
        _combine_kernel,
        grid_spec=grid_spec,
        out_shape=jax.ShapeDtypeStruct((t, d), F32),
        compiler_params=_cparams("arbitrary"),
    )(dst, y_sorted, route, x, gate.reshape(n_batch, 1, d), ln_g.reshape(1, d), ln_b.reshape(1, d))


def _source_rows_kernel(dst_ref, src_ref, *, n_rows):
    def clear(r, c):
        src_ref[r] = 0
        return c
    lax.fori_loop(0, n_rows, clear, 0, unroll=8)

    def place(a, c):
        src_ref[dst_ref[a]] = a // TOPK_IN_GROUP
        return c
    lax.fori_loop(0, dst_ref.shape[0], place, 0, unroll=8)


def _source_rows(dst, n_rows):
    return pl.pallas_call(
        functools.partial(_source_rows_kernel, n_rows=n_rows),
        in_specs=[pl.BlockSpec(memory_space=pltpu.SMEM)],
        out_specs=pl.BlockSpec(memory_space=pltpu.SMEM),
        out_shape=jax.ShapeDtypeStruct((n_rows,), jnp.int32),
    )(dst)


def _sort_plan(route, seq):
    t = route.shape[0]
    tm = MOE_TILE
    eid = route[:, :TOPK_IN_GROUP].astype(jnp.int32).reshape(-1)
    onehot = (eid[:, None] == jnp.arange(N_EXPERTS)[None, :]).astype(jnp.int32)
    before = jnp.cumsum(onehot, axis=0) - onehot
    rank = jnp.sum(before * onehot, axis=1)
    counts = jnp.sum(onehot, axis=0)
    tiles = (counts + tm - 1) // tm
    tile_end = jnp.cumsum(tiles)
    tile_start = tile_end - tiles
    dst = tile_start[eid] * tm + rank
    n_tiles = (t * TOPK_IN_GROUP) // tm + N_EXPERTS
    tile_ids = jnp.arange(n_tiles)
    tile_expert = jnp.sum((tile_ids[:, None] >= tile_end[None, :]).astype(jnp.int32), axis=1)
    tile_expert = jnp.minimum(tile_expert, N_EXPERTS - 1)
    n_used = tile_end[-1:].astype(jnp.int32)
    dst = dst.astype(jnp.int32)
    src_tok = _source_rows(dst, n_tiles * tm)
    row_batch = (src_tok // seq).reshape(-1, 1)
    return dst, tile_expert.astype(jnp.int32), src_tok, n_used, row_batch


def _moe_layer(x, x_tm, scale, shift, gate, w_group, b_group, w_router, b_router, w1, w3, w2, layer,
               ln_g, ln_b, seq):
    route = _router(x, scale, shift, w_group, b_group, w_router, b_router, seq)
    dst, tile_expert, src_tok, n_used, row_batch = _sort_plan(route, seq)
    y_sorted = _experts(x_tm, scale, shift, tile_expert, src_tok, n_used, row_batch, w1, w3, w2,
                        layer)
    return _combine_ln(dst, y_sorted, route, x, gate, ln_g, ln_b, seq)


def _swa_layer(x, scale, shift, gate, tables, w_qkv, b_qkv, sinks, w_o, ln_g, ln_b, seq):
    qw = Q_HEADS * HEAD_DIM
    kvw = KV_HEADS * HEAD_DIM
    tn = 256
    qkv = _mod_proj(x, scale, shift, w_qkv.astype(BF16), b_qkv, tables, seq,
                    rope_lo=qw // tn, rope_hi=(qw + kvw) // tn, tn=tn)
    o = _band_attention(qkv, 0, qw // kvw, qw // kvw + 1, tables, seq, SWA_WINDOW, sinks=sinks)
    return _out_proj_ln([o], w_o.astype(BF16), x, gate, ln_g, ln_b, seq)


def _nsa_layer(x, scale, shift, gate, tables, w_in, pe_k, pe_v, phi_k1, phi_k2, phi_v1, phi_v2,
               w_o, ln_g, ln_b, seq):
    t, d = x.shape
    n_batch = t // seq
    qw = Q_HEADS * HEAD_DIM
    kvw = KV_HEADS * HEAD_DIM
    tn = 256
    cols = [qw + i * kvw for i in range(7)]
    w_q, w_kc, w_vc, w_ks, w_vs, w_kw, w_vw, w_g = jnp.split(w_in, cols, axis=1)
    w_g = w_g.reshape(d, Q_HEADS, 3).transpose(0, 2, 1).reshape(d, 3 * Q_HEADS)
    w_g = jnp.pad(w_g, ((0, 0), (0, tn - 3 * Q_HEADS)))
    w_all = jnp.concatenate([w_q, w_ks, w_kw, w_vs, w_vw, w_kc, w_vc, w_g], axis=1).astype(BF16)
    n_cols = w_all.shape[1]
    proj = _mod_proj(x, scale, shift, w_all, jnp.zeros((n_cols,), F32), tables, seq,
                     rope_lo=qw // tn, rope_hi=(qw + 2 * kvw) // tn, tn=tn)
    kv_blk = qw // kvw
    k_s, v_s, k_c, v_c = (proj[:, qw + i * kvw: qw + (i + 1) * kvw] for i in (0, 2, 4, 5))
    gate_blk = (qw + 6 * kvw) // LANES

    kc = _compress(k_c, pe_k, phi_k1, phi_k2, seq)
    vc = _compress(v_c, pe_v, phi_v1, phi_v2, seq)
    o_cmp, selb = _cmp_attention(proj, kc, vc, gate_blk, seq, gate_col=0)

    heads = lambda a: a.reshape(n_batch, seq, KV_HEADS, HEAD_DIM).transpose(0, 2, 1, 3)
    onehot = (jnp.arange(seq)[:, None] // SEL_BLOCK == jnp.arange(HEAD_DIM)[None, :]).astype(BF16)
    onehot = jnp.broadcast_to(onehot, (n_batch, KV_HEADS, seq, HEAD_DIM))
    ks_h, vs_h = heads(k_s), heads(v_s)
    ka_even = jnp.concatenate([ks_h, onehot], axis=-1)
    ka_odd = jnp.concatenate([onehot, ks_h], axis=-1)
    vs2 = jnp.concatenate([vs_h, vs_h], axis=-1)
    o_sel = _sel_attention(proj, selb, ka_even, ka_odd, vs2, tables, gate_blk, seq, gate_col=Q_HEADS)
    o_win = _band_attention(proj, 0, kv_blk + 1, kv_blk + 3, tables, seq, NSA_WINDOW,
                            gate_blk=gate_blk, gate_col=2 * Q_HEADS)
    return _out_proj_ln([o_cmp, o_sel, o_win], w_o.astype(BF16), x, gate, ln_g, ln_b, seq)


def kernel(x, c, positions, w_ada, b_ada, swa_w_qkv, swa_b_qkv, swa_sinks, swa_w_o, nsa_w_in,
           nsa_pe_k, nsa_pe_v, nsa_phi_k1, nsa_phi_k2, nsa_phi_v1, nsa_phi_v2, nsa_w_o,
           moe_w_group, moe_b_group, moe_w_router, moe_b_router, moe_w1, moe_w3, moe_w2,
           ln_t_g, ln_t_b, ln_c_g, ln_c_b):
    n_batch, seq, d = x.shape
    depth = w_ada.shape[0]
    xt = x.reshape(n_batch * seq, d)
    mod = _adaln_mod(c, w_ada, b_ada)
    tables = _rope_tables(positions)
    for i in range(depth):
        sh_t, sc_t, g_t, sh_c, sc_c, g_c = (mod[i, :, k * d:(k + 1) * d] for k in range(6))
        j = i // 2
        if i % 2 == 0:
            xt, x_tm = _swa_layer(xt, sc_t, sh_t, g_t, tables, swa_w_qkv[j], swa_b_qkv[j], swa_sinks[j],
                            swa_w_o[j], ln_t_g[i], ln_t_b[i], seq)
        else:
            xt, x_tm = _nsa_layer(xt, sc_t, sh_t, g_t, tables, nsa_w_in[j], nsa_pe_k[j], nsa_pe_v[j],
                            nsa_phi_k1[j], nsa_phi_k2[j], nsa_phi_v1[j], nsa_phi_v2[j], nsa_w_o[j],
                            ln_t_g[i], ln_t_b[i], seq)
        xt = _moe_layer(xt, x_tm, sc_c, sh_c, g_c, moe_w_group[i], moe_b_group[i], moe_w_router[i],
                        moe_b_router[i], moe_w1, moe_w3, moe_w2, i, ln_c_g[i], ln_c_b[i], seq)
    return xt.reshape(n_batch, seq, d)
```

```python
import functools

import jax
import jax.numpy as jnp
from jax import lax
from jax.experimental import pallas as pl
from jax.experimental.pallas import tpu as pltpu

F32 = jnp.float32
BF16 = jnp.bfloat16

HEAD_DIM = 64
ROPE_DIM = HEAD_DIM // 4
ROPE_HALF = ROPE_DIM // 2
ROPE_THETA = 500000.0
Q_HEADS = 32
KV_HEADS = 4
GROUP = Q_HEADS // KV_HEADS
SWA_WINDOW = 128
NSA_WINDOW = 512
CMP_BLOCK = 32
CMP_STRIDE = 16
SEL_BLOCK = 64
SEL_TOPK = 16
FORCE_BONUS = 1e4
N_GROUPS = 4
EXPERTS_PER_GROUP = 4
N_EXPERTS = N_GROUPS * EXPERTS_PER_GROUP
TOPK_IN_GROUP = 2
DEPTH = 2
ALPHA = (2 * DEPTH) ** 0.25
LN_EPS = 1e-5
NEG_INF = -1e30
LOG2E = 1.4426950408889634

LANES = 128
ATTN_BLOCK = 128
VMEM_LIMIT = 56 * 1024 * 1024


def _cparams(*sem):
    return pltpu.CompilerParams(dimension_semantics=sem, vmem_limit_bytes=VMEM_LIMIT)


def _mod_kernel(cb_ref, w_ref, b_ref, o_ref, cs_ref, *, n_batch, tn):
    @pl.when((pl.program_id(0) == 0) & (pl.program_id(1) == 0))
    def _():
        c = cb_ref[...]
        cs_ref[...] = c * (1.0 / (1.0 + jnp.exp(-c)))

    for b in range(n_batch):
        cs = cs_ref[b]
        parts = []
        for g in range(tn // LANES):
            wg = w_ref[0, :, g * LANES:(g + 1) * LANES]
            parts.append(jnp.sum(wg * cs, axis=0, keepdims=True))
        o_ref[0, b:b + 1, :] = jnp.concatenate(parts, axis=-1) + b_ref[0]


def _adaln_mod(c, w_ada, b_ada):
    n_batch, d = c.shape
    depth, _, n = w_ada.shape
    tn = 512
    cb = jnp.broadcast_to(c[:, :, None], (n_batch, d, LANES))
    return pl.pallas_call(
        functools.partial(_mod_kernel, n_batch=n_batch, tn=tn),
        grid=(depth, n // tn),
        in_specs=[
            pl.BlockSpec((n_batch, d, LANES), lambda l, j: (0, 0, 0)),
            pl.BlockSpec((1, d, tn), lambda l, j: (l, 0, j)),
            pl.BlockSpec((1, 1, tn), lambda l, j: (l, 0, j)),
        ],
        out_specs=pl.BlockSpec((1, n_batch, tn), lambda l, j: (l, 0, j)),
        out_shape=jax.ShapeDtypeStruct((depth, n_batch, n), F32),
        scratch_shapes=[pltpu.VMEM((n_batch, d, LANES), F32)],
        compiler_params=_cparams("arbitrary", "arbitrary"),
    )(cb, w_ada, b_ada.reshape(depth, 1, n))


def _rope_table_kernel(pos_ref, inv_ref, c_ref, s1_ref, s2_ref):
    ang = pos_ref[...] * inv_ref[...]
    d = lax.broadcasted_iota(jnp.int32, ang.shape, 1) % HEAD_DIM
    cos = jnp.cos(ang)
    sin = jnp.sin(ang)
    c_ref[...] = jnp.where(d < ROPE_DIM, cos, 1.0)
    s1_ref[...] = jnp.where(d < ROPE_HALF, -sin, 0.0)
    s2_ref[...] = jnp.where((d >= ROPE_HALF) & (d < ROPE_DIM), sin, 0.0)


def _rope_tables(positions):
    t = positions.size
    tm = 1024
    inv_freq = ROPE_THETA ** (-jnp.arange(0, ROPE_DIM, 2, dtype=F32) / ROPE_DIM)
    lane = jnp.arange(LANES) % HEAD_DIM
    inv_lane = inv_freq[lane % ROPE_HALF].reshape(1, LANES)
    pos = positions.astype(F32).reshape(t, 1)
    spec = pl.BlockSpec((tm, LANES), lambda i: (i, 0))
    return pl.pallas_call(
        _rope_table_kernel,
        grid=(t // tm,),
        in_specs=[pl.BlockSpec((tm, 1), lambda i: (i, 0)),
                  pl.BlockSpec((1, LANES), lambda i: (0, 0))],
        out_specs=[spec, spec, spec],
        out_shape=[jax.ShapeDtypeStruct((t, LANES), F32)] * 3,
        compiler_params=_cparams("arbitrary"),
    )(pos, inv_lane)


def _apply_rope(x, c, s1, s2):
    reps = x.shape[-1] // LANES
    if reps > 1:
        c = jnp.concatenate([c] * reps, axis=-1)
        s1 = jnp.concatenate([s1] * reps, axis=-1)
        s2 = jnp.concatenate([s2] * reps, axis=-1)
    n = x.shape[-1]
    up = pltpu.roll(x, n - ROPE_HALF, 1)
    down = pltpu.roll(x, ROPE_HALF, 1)
    return x * c + up * s1 + down * s2


def _proj_kernel(x_ref, sc_ref, sh_ref, w_ref, b_ref, c_ref, s1_ref, s2_ref, o_ref, h_ref,
                 *, rope_lo, rope_hi, n_tiles):
    j = pl.program_id(1)
    groups = o_ref.shape[1] // LANES

    @pl.when(j == 0)
    def _():
        h_ref[...] = (x_ref[...] * (1.0 + sc_ref[0]) + sh_ref[0]).astype(BF16)

    acc = jnp.dot(h_ref[...], w_ref[...], preferred_element_type=F32) + b_ref[...]

    def write(pattern):
        for g in range(groups):
            piece = acc[:, g * LANES:(g + 1) * LANES]
            if pattern[g]:
                piece = _apply_rope(piece, c_ref[...], s1_ref[...], s2_ref[...])
            o_ref[:, g * LANES:(g + 1) * LANES] = piece.astype(o_ref.dtype)

    patterns = {}
    for tile in range(n_tiles):
        pattern = tuple(rope_lo <= tile * groups + g < rope_hi for g in range(groups))
        patterns.setdefault(pattern, []).append(tile)
    for pattern, tiles in patterns.items():
        cond = j == tiles[0]
        for tile in tiles[1:]:
            cond = cond | (j == tile)
        pl.when(cond)(functools.partial(write, pattern))


def _mod_proj(x, scale, shift, w, bias, tables, seq, *, rope_lo=0, rope_hi=0, tn=256, tm=1024,
              out_dtype=BF16):
    t, d = x.shape
    n = w.shape[1]
    n_batch = scale.shape[0]
    tm = min(tm, seq)
    per_batch = seq // tm
    c_tab, s1_tab, s2_tab = tables
    vec = pl.BlockSpec((1, 1, d), lambda i, j: (i // per_batch, 0, 0))
    tab = pl.BlockSpec((tm, LANES), lambda i, j: (i, 0))
    return pl.pallas_call(
        functools.partial(_proj_kernel, rope_lo=rope_lo, rope_hi=rope_hi, n_tiles=n // tn),
        grid=(t // tm, n // tn),
        in_specs=[
            pl.BlockSpec((tm, d), lambda i, j: (i, 0)),
            vec, vec,
            pl.BlockSpec((d, tn), lambda i, j: (0, j)),
            pl.BlockSpec((1, tn), lambda i, j: (0, j)),
            tab, tab, tab,
        ],
        out_specs=pl.BlockSpec((tm, tn), lambda i, j: (i, j)),
        out_shape=jax.ShapeDtypeStruct((t, n), out_dtype),
        scratch_shapes=[pltpu.VMEM((tm, d), BF16)],
        compiler_params=_cparams("arbitrary", "arbitrary"),
    )(x, scale.reshape(n_batch, 1, d), shift.reshape(n_batch, 1, d), w, bias.reshape(1, n),
      c_tab, s1_tab, s2_tab)


def _scaled_q(q_ref, tables=None):
    q = q_ref[...].astype(F32)
    if tables is not None:
        q = _apply_rope(q, *(t[...] for t in tables))
    return q * (HEAD_DIM ** -0.5 * LOG2E)


def _split_even_odd(q):
    even = (lax.broadcasted_iota(jnp.int32, q.shape, 1) % LANES) < HEAD_DIM
    return jnp.where(even, q, 0.0).astype(BF16), jnp.where(even, 0.0, q).astype(BF16)


def _stack_heads(q_even, q_odd, first_head):
    parts = []
    for g in range(GROUP):
        head = first_head + g
        src = q_even if head % 2 == 0 else q_odd
        parts.append(src[:, (head // 2) * LANES:(head // 2 + 1) * LANES])
    return jnp.concatenate(parts, axis=0)


def _twice(x):
    return jnp.concatenate([x, x], axis=-1)


def _merge_pairs(o, rows, scale):
    even = lax.broadcasted_iota(jnp.int32, (rows, LANES), 1) < HEAD_DIM
    out = []
    for g in range(0, GROUP, 2):
        a = o[g * rows:(g + 1) * rows] * scale[g]
        b = o[(g + 1) * rows:(g + 2) * rows] * scale[g + 1]
        out.append(jnp.where(even, a, b))
    return jnp.concatenate(out, axis=-1)


def _sigmoid(x):
    return 1.0 / (1.0 + jnp.exp(-x))


def _band_attn_kernel(*refs, window, use_sinks, use_gate, gate_col):
    it = iter(refs)
    q_ref, k_ref, v_ref, c_ref, s1_ref, s2_ref = (next(it) for _ in range(6))
    sink_ref = next(it) if use_sinks else None
    gate_ref = next(it) if use_gate else None
    o_ref = next(it)

    n = pl.program_id(1)
    rows = ATTN_BLOCK
    halo = -(-window // rows) * rows
    span = rows + halo
    start = jnp.maximum(n * rows - halo, 0)
    start = pl.multiple_of(start, rows)
    q_even, q_odd = _split_even_odd(_scaled_q(q_ref, (c_ref, s1_ref, s2_ref)))
    k = k_ref[pl.ds(start, span), :]
    v = v_ref[pl.ds(start, span), :]
    qpos = n * rows + lax.broadcasted_iota(jnp.int32, (rows, span), 0)
    kpos = start + lax.broadcasted_iota(jnp.int32, (rows, span), 1)
    rel = qpos - kpos
    mask = (rel >= 0) & (rel < window)
    if use_gate:
        gate = _sigmoid(gate_ref[...].astype(F32))

    for j in range(KV_HEADS):
        k2 = _twice(k[:, j * HEAD_DIM:(j + 1) * HEAD_DIM])
        v2 = _twice(v[:, j * HEAD_DIM:(j + 1) * HEAD_DIM])
        q8 = _stack_heads(q_even, q_odd, j * GROUP)
        s = lax.dot_general(q8, k2, (((1,), (1,)), ((), ())), preferred_element_type=F32)
        ps, scale = [], []
        for g in range(GROUP):
            sg = jnp.where(mask, s[g * rows:(g + 1) * rows], NEG_INF)
            m = jnp.max(sg, axis=-1, keepdims=True)
            if use_sinks:
                sink = sink_ref[j * GROUP + g] * LOG2E
                m = jnp.maximum(m, sink)
            e = jnp.exp2(sg - m)
            den = jnp.sum(e, axis=-1, keepdims=True)
            if use_sinks:
                den = den + jnp.exp2(sink - m)
            inv = 1.0 / den
            if use_gate:
                col = gate_col + j * GROUP + g
                inv = inv * gate[:, col:col + 1]
            ps.append(e.astype(BF16))
            scale.append(inv)
        o = jnp.dot(jnp.concatenate(ps, axis=0), v2, preferred_element_type=F32)
        o_ref[:, j * GROUP * HEAD_DIM:(j + 1) * GROUP * HEAD_DIM] = (
            _merge_pairs(o, rows, scale).astype(o_ref.dtype))


def _band_attention(proj, q_blk, k_blk, v_blk, tables, seq, window, sinks=None, gate_blk=None,
                    gate_col=0):
    t = proj.shape[0]
    n_batch = t // seq
    nb = seq // ATTN_BLOCK
    qw = Q_HEADS * HEAD_DIM
    kvw = KV_HEADS * HEAD_DIM
    row = lambda w, c: pl.BlockSpec((ATTN_BLOCK, w), lambda b, i: (b * nb + i, c))
    res = lambda c: pl.BlockSpec((seq, kvw), lambda b, i: (b, c))
    in_specs = [row(qw, q_blk), res(k_blk), res(v_blk), row(LANES, 0), row(LANES, 0), row(LANES, 0)]
    args = [proj, proj, proj, *tables]
    if sinks is not None:
        in_specs.append(pl.BlockSpec(memory_space=pltpu.SMEM))
        args.append(sinks)
    if gate_blk is not None:
        in_specs.append(row(LANES, gate_blk))
        args.append(proj)
    return pl.pallas_call(
        functools.partial(_band_attn_kernel, window=window, use_sinks=sinks is not None,
                          use_gate=gate_blk is not None, gate_col=gate_col),
        grid=(n_batch, nb),
        in_specs=in_specs,
        out_specs=row(qw, 0),
        out_shape=jax.ShapeDtypeStruct((t, qw), BF16),
        compiler_params=_cparams("arbitrary", "arbitrary"),
    )(*args)


def _layer_norm(y, g, b):
    mu = jnp.mean(y, axis=-1, keepdims=True)
    yc = y - mu
    var = jnp.mean(yc * yc, axis=-1, keepdims=True)
    return yc * lax.rsqrt(var + LN_EPS) * g + b


def _out_proj_kernel(*refs, n_parts):
    o_parts = refs[:n_parts]
    w_ref, x_ref, gate_ref, g_ref, b_ref, out_ref, out_tm_ref = refs[n_parts:]
    if n_parts == 1:
        o = o_parts[0][...]
    else:
        acc = o_parts[0][...].astype(F32)
        for r in o_parts[1:]:
            acc = acc + r[...].astype(F32)
        o = acc.astype(BF16)
    mix = jnp.dot(o, w_ref[...], preferred_element_type=F32)
    y = ALPHA * x_ref[...] + gate_ref[0] * mix
    res = _layer_norm(y, g_ref[...], b_ref[...])
    out_ref[...] = res
    _store_token_major(out_tm_ref, res)


def _out_proj_ln(o_parts, w_o, x, gate, ln_g, ln_b, seq, tm=256):
    t, d = x.shape
    n = d // LANES
    n_batch = gate.shape[0]
    per_batch = seq // tm
    k = w_o.shape[0]
    row_o = pl.BlockSpec((tm, k), lambda i: (i, 0))
    row_x = pl.BlockSpec((tm, d), lambda i: (i, 0))
    vec = pl.BlockSpec((1, d), lambda i: (0, 0))
    return pl.pallas_call(
        functools.partial(_out_proj_kernel, n_parts=len(o_parts)),
        grid=(t // tm,),
        in_specs=[row_o] * len(o_parts) + [
            pl.BlockSpec((k, d), lambda i: (0, 0)),
            row_x,
            pl.BlockSpec((1, 1, d), lambda i: (i // per_batch, 0, 0)),
            vec, vec],
        out_specs=[row_x, pl.BlockSpec((tm * n, LANES), lambda i: (i, 0))],
        out_shape=[jax.ShapeDtypeStruct((t, d), F32), jax.ShapeDtypeStruct((t * n, LANES), F32)],
        compiler_params=_cparams("arbitrary"),
    )(*o_parts, w_o, x, gate.reshape(n_batch, 1, d), ln_g.reshape(1, d), ln_b.reshape(1, d))


def _compress_kernel(x_ref, pe_ref, w1_ref, w1a_ref, w1b_ref, w2_ref, o_ref):
    x = x_ref[0]
    a = jnp.dot(x, w1a_ref[...], preferred_element_type=F32)
    b = jnp.dot(x, w1b_ref[...], preferred_element_type=F32)
    nc = a.shape[0]
    b_next = pltpu.roll(b, nc - 1, 0)
    pe_term = jnp.dot(pe_ref[...], w1_ref[...], preferred_element_type=F32)[0:1]
    hid = a + b_next + jnp.concatenate([pe_term] * KV_HEADS, axis=-1)
    hid = hid * (1.0 / (1.0 + jnp.exp(-hid)))
    o_ref[0] = jnp.dot(hid.astype(BF16), w2_ref[...], preferred_element_type=F32).astype(o_ref.dtype)


def _block_diag_heads(w):
    p, d, n = w.shape
    eye = jnp.eye(KV_HEADS, dtype=w.dtype)
    big = w[:, None, :, None, :] * eye[None, :, None, :, None]
    return big.reshape(p * KV_HEADS * d, KV_HEADS * n)


def _compress(xc, pe, w1, w2, seq):
    t, kvw = xc.shape
    n_batch = t // seq
    nch = seq // CMP_STRIDE
    hidden = w1.shape[1]
    x = xc.reshape(n_batch, nch, CMP_STRIDE * kvw)
    w1r = w1.reshape(CMP_BLOCK, HEAD_DIM, hidden)
    w1a = _block_diag_heads(w1r[:CMP_STRIDE]).astype(BF16)
    w1b = _block_diag_heads(w1r[CMP_STRIDE:]).astype(BF16)
    eye = jnp.eye(KV_HEADS, dtype=w2.dtype)
    w2d = (w2[None, :, None, :] * eye[:, None, :, None]).reshape(KV_HEADS * hidden, kvw).astype(BF16)
    pe8 = jnp.broadcast_to(pe.reshape(1, CMP_BLOCK * HEAD_DIM), (8, CMP_BLOCK * HEAD_DIM)).astype(BF16)
    full = lambda a: pl.BlockSpec(a.shape, lambda b: (0,) * a.ndim)
    w1b16 = w1.astype(BF16)
    return pl.pallas_call(
        _compress_kernel,
        grid=(n_batch,),
        in_specs=[pl.BlockSpec((1, nch, CMP_STRIDE * kvw), lambda b: (b, 0, 0)),
                  full(pe8), full(w1b16), full(w1a), full(w1b), full(w2d)],
        out_specs=pl.BlockSpec((1, nch, kvw), lambda b: (b, 0, 0)),
        out_shape=jax.ShapeDtypeStruct((n_batch, nch, kvw), BF16),
        compiler_params=_cparams("arbitrary"),
    )(x, pe8, w1b16, w1a, w1b, w2d)


def _topk_bias(score, top_k):
    nblk, rows = score.shape
    sub = lax.broadcasted_iota(jnp.int32, (8, rows), 0)
    chunks = [score[c * 8:(c + 1) * 8] for c in range(nblk // 8)]
    counts = [jnp.zeros((8, rows), F32) for _ in chunks]
    for sp in range(nblk):
        row = score[sp:sp + 1]
        for c, chunk in enumerate(chunks):
            ge = jnp.where(row >= chunk, 1.0, 0.0)
            gt = jnp.where(row > chunk, 1.0, 0.0)
            if sp < c * 8:
                beats = ge
            elif sp >= (c + 1) * 8:
                beats = gt
            else:
                beats = jnp.where(sub > sp - c * 8, ge, gt)
            counts[c] = counts[c] + beats
    return jnp.concatenate([jnp.where(cnt < top_k, 0.0, NEG_INF) for cnt in counts], axis=0)


def _cmp_attn_kernel(q_ref, kc_ref, vc_ref, ovt_ref, gate_ref, o_ref, sel_ref, *, gate_col, top_k):
    n = pl.program_id(1)
    rows = ATTN_BLOCK
    nc = kc_ref.shape[1]
    nblk = ovt_ref.shape[0]
    q_even, q_odd = _split_even_odd(_scaled_q(q_ref))
    kc = kc_ref[0]
    vc = vc_ref[0]
    t = n * rows + lax.broadcasted_iota(jnp.int32, (rows, nc), 0)
    cmp_end = lax.broadcasted_iota(jnp.int32, (rows, nc), 1) * CMP_STRIDE + (CMP_BLOCK - 1)
    valid = cmp_end <= t
    has_valid = n * rows + lax.broadcasted_iota(jnp.int32, (rows, 1), 0) >= CMP_BLOCK - 1
    gate = _sigmoid(gate_ref[...].astype(F32))

    ts = n * rows + lax.broadcasted_iota(jnp.int32, (nblk, rows), 1)
    blk = lax.broadcasted_iota(jnp.int32, (nblk, rows), 0)
    cur = lax.shift_right_arithmetic(ts, SEL_BLOCK.bit_length() - 1)
    causal = blk * SEL_BLOCK <= ts
    forced = (blk == 0) | (blk == cur) | (blk == cur - 1)
    bonus = jnp.where(forced, FORCE_BONUS, 0.0)

    for j in range(KV_HEADS):
        k2 = _twice(kc[:, j * HEAD_DIM:(j + 1) * HEAD_DIM])
        v2 = _twice(vc[:, j * HEAD_DIM:(j + 1) * HEAD_DIM])
        q8 = _stack_heads(q_even, q_odd, j * GROUP)
        s = lax.dot_general(q8, k2, (((1,), (1,)), ((), ())), preferred_element_type=F32)
        ps, scale, psum = [], [], None
        for g in range(GROUP):
            sg = jnp.where(valid, s[g * rows:(g + 1) * rows], NEG_INF)
            m = jnp.max(sg, axis=-1, keepdims=True)
            e = jnp.exp2(sg - m)
            den = jnp.sum(e, axis=-1, keepdims=True)
            inv = jnp.where(has_valid, 1.0 / den, 0.0)
            pn = e * inv
            psum = pn if psum is None else psum + pn
            col = gate_col + j * GROUP + g
            ps.append(e.astype(BF16))
            scale.append(inv * gate[:, col:col + 1])
        o = jnp.dot(jnp.concatenate(ps, axis=0), v2, preferred_element_type=F32)
        o_ref[:, j * GROUP * HEAD_DIM:(j + 1) * GROUP * HEAD_DIM] = (
            _merge_pairs(o, rows, scale).astype(o_ref.dtype))

        imp_t = lax.dot_general(ovt_ref[...], psum, (((1,), (1,)), ((), ())),
                                preferred_element_type=F32,
                                precision=lax.Precision.HIGHEST)
        score = jnp.where(causal, imp_t + bonus, NEG_INF)
        bias_t = _topk_bias(score, top_k)
        sel_ref[0, j] = jnp.concatenate([bias_t, bias_t], axis=0).T.astype(sel_ref.dtype)


def _cmp_attention(proj, kc, vc, gate_blk, seq, gate_col):
    t = proj.shape[0]
    n_batch = t // seq
    nb = seq // ATTN_BLOCK
    nc = kc.shape[1]
    nsel = seq // SEL_BLOCK
    assert nsel <= HEAD_DIM
    qw = Q_HEADS * HEAD_DIM
    top_k = min(SEL_TOPK, nsel)
    c0 = jnp.arange(nc)[None, :] * CMP_STRIDE
    s0 = jnp.arange(HEAD_DIM)[:, None] * SEL_BLOCK
    ov = jnp.clip(jnp.minimum(c0 + CMP_BLOCK, s0 + SEL_BLOCK) - jnp.maximum(c0, s0), 0)
    overlap_t = ov.astype(F32) / CMP_BLOCK
    row = lambda w, c: pl.BlockSpec((ATTN_BLOCK, w), lambda b, i: (b * nb + i, c))
    res = pl.BlockSpec((1, nc, kc.shape[2]), lambda b, i: (b, 0, 0))
    return pl.pallas_call(
        functools.partial(_cmp_attn_kernel, gate_col=gate_col, top_k=top_k),
        grid=(n_batch, nb),
        in_specs=[row(qw, 0), res, res,
                  pl.BlockSpec((HEAD_DIM, nc), lambda b, i: (0, 0)), row(LANES, gate_blk)],
        out_specs=[row(qw, 0),
                   pl.BlockSpec((1, KV_HEADS, ATTN_BLOCK, LANES), lambda b, i: (b, 0, i, 0))],
        out_shape=[jax.ShapeDtypeStruct((t, qw), BF16),
                   jax.ShapeDtypeStruct((n_batch, KV_HEADS, seq, LANES), BF16)],
        compiler_params=_cparams("arbitrary", "arbitrary"),
    )(proj, kc, vc, overlap_t, proj)


SEL_TILE = 512


def _sel_attn_kernel(q_ref, sel_ref, kae_ref, kao_ref, v_ref, c_ref, s1_ref, s2_ref, gate_ref, o_ref,
                     m_ref, l_ref, acc_ref, *, gate_col):
    j = pl.program_id(1)
    n = pl.program_id(2)
    rows = ATTN_BLOCK
    tk = SEL_TILE
    pairs = GROUP // 2
    q = _scaled_q(q_ref, (c_ref, s1_ref, s2_ref))
    bias2 = sel_ref[0, 0].astype(F32)
    low = lax.broadcasted_iota(jnp.int32, (rows, LANES), 1) < HEAD_DIM
    lhs = []
    for parity in range(2):
        parts = []
        for i in range(pairs):
            slab = q[:, i * LANES:(i + 1) * LANES]
            parts.append(jnp.where(low, slab, bias2) if parity == 0 else jnp.where(low, bias2, slab))
        lhs.append(jnp.concatenate(parts, axis=0).astype(BF16))
    ka_refs = (kae_ref, kao_ref)

    m_ref[...] = jnp.full(m_ref.shape, NEG_INF, F32)
    l_ref[...] = jnp.zeros(l_ref.shape, F32)
    acc_ref[...] = jnp.zeros(acc_ref.shape, F32)

    def tile(kt, mask_bias):
        off = pl.multiple_of(kt * tk, tk)
        v2 = v_ref[0, 0, pl.ds(off, tk), :]
        scores = []
        for parity in range(2):
            ka = ka_refs[parity][0, 0, pl.ds(off, tk), :]
            s = lax.dot_general(lhs[parity], ka, (((1,), (1,)), ((), ())),
                                preferred_element_type=F32)
            scores.append(s if mask_bias is None else s + mask_bias)
        probs, alphas = [], []
        for parity in range(2):
            s = scores[parity]
            chunks = [s[:, c * LANES:(c + 1) * LANES] for c in range(tk // LANES)]
            rmax = chunks[0]
            for ch in chunks[1:]:
                rmax = jnp.maximum(rmax, ch)
            m_old = m_ref[parity]
            m_new = jnp.maximum(m_old, jnp.max(rmax, axis=-1, keepdims=True))
            alpha = jnp.exp2(m_old - m_new)
            ps = [jnp.exp2(ch - m_new) for ch in chunks]
            lsum = ps[0]
            for x in ps[1:]:
                lsum = lsum + x
            l_ref[parity] = l_ref[parity] * alpha + lsum
            m_ref[parity] = m_new
            probs.append(jnp.concatenate([x.astype(BF16) for x in ps], axis=-1))
            alphas.append(alpha)
        for parity in range(2):
            acc_ref[parity] = acc_ref[parity] * alphas[parity] + jnp.dot(
                probs[parity], v2, preferred_element_type=F32)

    def body(kt, carry):
        tile(kt, None)
        return carry

    n_full = (n * rows) // tk
    lax.fori_loop(0, n_full, body, 0)
    qpos = n * rows + lax.broadcasted_iota(jnp.int32, (rows, tk), 0)
    kpos = n_full * tk + lax.broadcasted_iota(jnp.int32, (rows, tk), 1)
    causal = jnp.where(kpos <= qpos, 0.0, NEG_INF)
    tile(n_full, jnp.concatenate([causal] * pairs, axis=0))

    gate = _sigmoid(gate_ref[...].astype(F32))
    lane = lax.broadcasted_iota(jnp.int32, gate.shape, 1)
    outs = []
    for i in range(pairs):
        scaled = []
        for parity in range(2):
            col = gate_col + j * GROUP + 2 * i + parity
            gcol = jnp.sum(jnp.where(lane == col, gate, 0.0), axis=-1, keepdims=True)
            l = jnp.sum(l_ref[parity, i * rows:(i + 1) * rows], axis=-1, keepdims=True)
            scaled.append(acc_ref[parity, i * rows:(i + 1) * rows] * (gcol / l))
        outs.append(jnp.where(low, scaled[0], scaled[1]))
    o_ref[...] = jnp.concatenate(outs, axis=-1).astype(o_ref.dtype)


def _sel_attention(proj, selb, ka_even, ka_odd, v2, tables, gate_blk, seq, gate_col):
    t = proj.shape[0]
    n_batch = t // seq
    nb = seq // ATTN_BLOCK
    gw = GROUP * HEAD_DIM
    pairs = GROUP // 2
    row = lambda c: pl.BlockSpec((ATTN_BLOCK, LANES), lambda b, j, i: (b * nb + i, c))
    res = pl.BlockSpec((1, 1, seq, LANES), lambda b, j, i: (b, j, 0, 0))
    acc = pltpu.VMEM((2, pairs * ATTN_BLOCK, LANES), F32)
    return pl.pallas_call(
        functools.partial(_sel_attn_kernel, gate_col=gate_col),
        grid=(n_batch, KV_HEADS, nb),
        in_specs=[
            pl.BlockSpec((ATTN_BLOCK, gw), lambda b, j, i: (b * nb + i, j)),
            pl.BlockSpec((1, 1, ATTN_BLOCK, LANES), lambda b, j, i: (b, j, i, 0)),
            res, res, res, row(0), row(0), row(0), row(gate_blk)],
        out_specs=pl.BlockSpec((ATTN_BLOCK, gw), lambda b, j, i: (b * nb + i, j)),
        out_shape=jax.ShapeDtypeStruct((t, Q_HEADS * HEAD_DIM), BF16),
        scratch_shapes=[acc, acc, acc],
        compiler_params=_cparams("arbitrary", "arbitrary", "arbitrary"),
    )(proj, selb, ka_even, ka_odd, v2, *tables, proj)


def _router_kernel(x_ref, sc_ref, sh_ref, w_ref, b_ref, o_ref):
    h = x_ref[...] * (1.0 + sc_ref[0]) + sh_ref[0]
    w = w_ref[...]
    h_hi = h.astype(BF16)
    h_lo = (h - h_hi.astype(F32)).astype(BF16)
    w_hi = w.astype(BF16)
    w_lo = (w - w_hi.astype(F32)).astype(BF16)
    logits = (jnp.dot(h_hi, w_hi, preferred_element_type=F32)
              + jnp.dot(h_lo, w_hi, preferred_element_type=F32)
              + jnp.dot(h_hi, w_lo, preferred_element_type=F32)) + b_ref[...]
    lane = lax.broadcasted_iota(jnp.int32, logits.shape, 1)
    big = 1 << 20

    gmask = lane < N_GROUPS
    lg = jnp.where(gmask, logits, -jnp.inf)
    eg = jnp.exp(lg - jnp.max(lg, axis=-1, keepdims=True))
    pg = eg / jnp.sum(eg, axis=-1, keepdims=True)
    g_prob = jnp.max(pg, axis=-1, keepdims=True)
    g_idx = jnp.min(jnp.where((pg == g_prob) & gmask, lane, big), axis=-1, keepdims=True)

    lo = N_GROUPS + g_idx * EXPERTS_PER_GROUP
    emask = (lane >= lo) & (lane < lo + EXPERTS_PER_GROUP)
    le = jnp.where(emask, logits, -jnp.inf)
    ee = jnp.exp(le - jnp.max(le, axis=-1, keepdims=True))
    pe = jnp.where(emask, ee / jnp.sum(ee, axis=-1, keepdims=True), -1.0)
    p1 = jnp.max(pe, axis=-1, keepdims=True)
    i1 = jnp.min(jnp.where(pe == p1, lane, big), axis=-1, keepdims=True)
    pe2 = jnp.where(lane == i1, -1.0, pe)
    p2 = jnp.max(pe2, axis=-1, keepdims=True)
    i2 = jnp.min(jnp.where(pe2 == p2, lane, big), axis=-1, keepdims=True)
    tot = p1 + p2
    w1 = g_prob * (p1 / tot)
    w2 = g_prob * (p2 / tot)
    e1 = (i1 - N_GROUPS).astype(F32)
    e2 = (i2 - N_GROUPS).astype(F32)
    o_ref[...] = jnp.where(lane == 0, e1, jnp.where(lane == 1, e2,
                           jnp.where(lane == 2, w1, jnp.where(lane == 3, w2, 0.0))))


def _router(x, scale, shift, w_group, b_group, w_router, b_router, seq, tm=512):
    t, d = x.shape
    n_batch = scale.shape[0]
    per_batch = seq // tm
    w = jnp.zeros((d, LANES), F32).at[:, :N_GROUPS].set(w_group)
    w = w.at[:, N_GROUPS:N_GROUPS + N_EXPERTS].set(w_router)
    b = jnp.zeros((1, LANES), F32).at[0, :N_GROUPS].set(b_group)
    b = b.at[0, N_GROUPS:N_GROUPS + N_EXPERTS].set(b_router)
    vec = pl.BlockSpec((1, 1, d), lambda i: (i // per_batch, 0, 0))
    return pl.pallas_call(
        _router_kernel,
        grid=(t // tm,),
        in_specs=[pl.BlockSpec((tm, d), lambda i: (i, 0)), vec, vec,
                  pl.BlockSpec((d, LANES), lambda i: (0, 0)),
                  pl.BlockSpec((1, LANES), lambda i: (0, 0))],
        out_specs=pl.BlockSpec((tm, LANES), lambda i: (i, 0)),
        out_shape=jax.ShapeDtypeStruct((t, LANES), F32),
        compiler_params=_cparams("arbitrary"),
    )(x, scale.reshape(n_batch, 1, d), shift.reshape(n_batch, 1, d), w, b)


MOE_TILE = 256


def _store_token_major(ref, val):
    rows, n = val.shape[0], val.shape[1] // LANES
    for a in range(n):
        ref[pl.ds(a, rows, stride=n), :] = val[:, a * LANES:(a + 1) * LANES]


def _load_token_major(ref, rows, n):
    return jnp.concatenate([ref[pl.ds(a, rows, stride=n), :] for a in range(n)], axis=-1)


def _expert_kernel(te_ref, src_ref, nt_ref, x_hbm, rowb_ref, sc_ref, sh_ref, w1_ref, w3_ref, w2_ref,
                   o_ref, xbuf, w1b, w3b, w2b, sem, *, n_batch):
    i = pl.program_id(0)
    n_used = nt_ref[0]
    tm = MOE_TILE
    n = sc_ref.shape[-1] // LANES

    def row_copy(tile_idx, slot, r):
        tok = src_ref[tile_idx * tm + r]
        return pltpu.make_async_copy(x_hbm.at[pl.ds(pl.multiple_of(tok * n, n), n), :],
                                     xbuf.at[slot, pl.ds(pl.multiple_of(r * n, n), n), :],
                                     sem.at[slot])

    def start_rows(tile_idx, slot, lo, hi):
        def body(r, c):
            row_copy(tile_idx, slot, r).start()
            return c
        lax.fori_loop(lo, hi, body, 0, unroll=8)

    def wait_tile(slot):
        pltpu.make_async_copy(x_hbm.at[pl.ds(0, tm * n), :], xbuf.at[slot], sem.at[slot]).wait()

    @pl.when((i == 0) & (n_used > 0))
    def _():
        start_rows(0, 0, 0, tm)

    @pl.when(i + 1 < n_used)
    def _():
        start_rows(i + 1, (i + 1) % 2, 0, tm)

    new_expert = (i == 0) | (te_ref[i] != te_ref[jnp.maximum(i - 1, 0)])

    @pl.when((i < n_used) & new_expert)
    def _():
        w1b[...] = w1_ref[0, 0].astype(BF16)
        w3b[...] = w3_ref[0, 0].astype(BF16)
        w2b[...] = w2_ref[0, 0].astype(BF16)

    @pl.when(i < n_used)
    def _():
        slot = i % 2
        wait_tile(slot)
        x = _load_token_major(xbuf.at[slot], tm, n)
        rowb = rowb_ref[...]
        h = x * (1.0 + sc_ref[0]) + sh_ref[0]
        for b in range(1, n_batch):
            h = jnp.where(rowb == b, x * (1.0 + sc_ref[b]) + sh_ref[b], h)
        hb = h.astype(BF16)
        a = jnp.dot(hb, w1b[...], preferred_element_type=F32)
        g = jnp.dot(hb, w3b[...], preferred_element_type=F32)
        he = (a * _sigmoid(a) * g).astype(BF16)
        _store_token_major(o_ref, jnp.dot(he, w2b[...], preferred_element_type=F32))

    @pl.when(i >= n_used)
    def _():
        o_ref[...] = jnp.zeros(o_ref.shape, o_ref.dtype)


def _experts(x_tm, scale, shift, tile_expert, src_tok, n_used, row_batch, w1, w3, w2, layer):
    n_batch, d = scale.shape
    n = d // LANES
    n_tiles = tile_expert.shape[0]
    de = w1.shape[3]
    tm = MOE_TILE
    grid_spec = pltpu.PrefetchScalarGridSpec(
        num_scalar_prefetch=3,
        grid=(n_tiles,),
        in_specs=[
            pl.BlockSpec(memory_space=pl.ANY),
            pl.BlockSpec((tm, 1), lambda i, te, src, nt: (i, 0)),
            pl.BlockSpec((n_batch, 1, d), lambda i, te, src, nt: (0, 0, 0)),
            pl.BlockSpec((n_batch, 1, d), lambda i, te, src, nt: (0, 0, 0)),
            pl.BlockSpec((1, 1, d, de), lambda i, te, src, nt: (layer, te[i], 0, 0)),
            pl.BlockSpec((1, 1, d, de), lambda i, te, src, nt: (layer, te[i], 0, 0)),
            pl.BlockSpec((1, 1, de, d), lambda i, te, src, nt: (layer, te[i], 0, 0)),
        ],
        out_specs=pl.BlockSpec((tm * n, LANES), lambda i, te, src, nt: (i, 0)),
        scratch_shapes=[pltpu.VMEM((2, tm * n, LANES), F32),
                        pltpu.VMEM((d, de), BF16), pltpu.VMEM((d, de), BF16),
                        pltpu.VMEM((de, d), BF16), pltpu.SemaphoreType.DMA((2,))],
    )
    return pl.pallas_call(
        functools.partial(_expert_kernel, n_batch=n_batch),
        grid_spec=grid_spec,
        out_shape=jax.ShapeDtypeStruct((n_tiles * tm * n, LANES), F32),
        compiler_params=_cparams("arbitrary"),
    )(tile_expert, src_tok, n_used, x_tm, row_batch, scale.reshape(n_batch, 1, d),
      shift.reshape(n_batch, 1, d), w1, w3, w2)


COMBINE_TILE = 256


def _combine_kernel(dst_ref, y_hbm, route_ref, x_ref, gate_ref, g_ref, b_ref, out_ref, ybuf, sem):
    i = pl.program_id(0)
    n_steps = pl.num_programs(0)
    tm = COMBINE_TILE

    n = x_ref.shape[-1] // LANES

    def row_copy(step, slot, k, r):
        pos = dst_ref[(step * tm + r) * TOPK_IN_GROUP + k]
        return pltpu.make_async_copy(
            y_hbm.at[pl.ds(pl.multiple_of(pos * n, n), n), :],
            ybuf.at[slot, pl.ds(pl.multiple_of((k * tm + r) * n, n), n), :], sem.at[slot])

    def start_step(step, slot):
        def body(r, c):
            for k in range(TOPK_IN_GROUP):
                row_copy(step, slot, k, r).start()
            return c
        lax.fori_loop(0, tm, body, 0, unroll=4)

    def wait_step(slot):
        pltpu.make_async_copy(y_hbm.at[pl.ds(0, TOPK_IN_GROUP * tm * n), :], ybuf.at[slot],
                              sem.at[slot]).wait()

    @pl.when(i == 0)
    def _():
        start_step(0, 0)

    @pl.when(i + 1 < n_steps)
    def _():
        start_step(i + 1, (i + 1) % 2)

    slot = i % 2
    wait_step(slot)
    route = route_ref[...]
    lane = lax.broadcasted_iota(jnp.int32, route.shape, 1)
    mix = None
    for k in range(TOPK_IN_GROUP):
        wk = jnp.sum(jnp.where(lane == TOPK_IN_GROUP + k, route, 0.0), axis=-1, keepdims=True)
        term = wk * _load_token_major(ybuf.at[slot, pl.ds(k * tm * n, tm * n), :], tm, n)
        mix = term if mix is None else mix + term
    y = ALPHA * x_ref[...] + gate_ref[0] * mix
    out_ref[...] = _layer_norm(y, g_ref[...], b_ref[...])


def _combine_ln(dst, y_sorted, route, x, gate, ln_g, ln_b, seq):
    t, d = x.shape
    n_batch = gate.shape[0]
    tm = COMBINE_TILE
    per_batch = seq // tm
    row = lambda w: pl.BlockSpec((tm, w), lambda i, dst: (i, 0))
    vec = pl.BlockSpec((1, d), lambda i, dst: (0, 0))
    grid_spec = pltpu.PrefetchScalarGridSpec(
        num_scalar_prefetch=1,
        grid=(t // tm,),
        in_specs=[pl.BlockSpec(memory_space=pl.ANY), row(LANES), row(d),
                  pl.BlockSpec((1, 1, d), lambda i, dst: (i // per_batch, 0, 0)), vec, vec],
        out_specs=row(d),
        scratch_shapes=[pltpu.VMEM((2, TOPK_IN_GROUP * tm * (d // LANES), LANES), F32),
                        pltpu.SemaphoreType.DMA((2,))],
    )
    return pl.pallas_call(
        _combine_kernel,
        grid_spec=grid_spec,
        out_shape=jax.ShapeDtypeStruct((t, d), F32),
        compiler_params=_cparams("arbitrary"),
    )(dst, y_sorted, route, x, gate.reshape(n_batch, 1, d), ln_g.reshape(1, d), ln_b.reshape(1, d))


def _sort_plan(route, seq):
    t = route.shape[0]
    tm = MOE_TILE
    eid = route[:, :TOPK_IN_GROUP].astype(jnp.int32).reshape(-1)
    onehot = (eid[:, None] == jnp.arange(N_EXPERTS)[None, :]).astype(jnp.int32)
    before = jnp.cumsum(onehot, axis=0) - onehot
    rank = jnp.sum(before * onehot, axis=1)
    counts = jnp.sum(onehot, axis=0)
    tiles = (counts + tm - 1) // tm
    tile_end = jnp.cumsum(tiles)
    tile_start = tile_end - tiles
    dst = tile_start[eid] * tm + rank
    n_tiles = (t * TOPK_IN_GROUP) // tm + N_EXPERTS
    tile_ids = jnp.arange(n_tiles)
    tile_expert = jnp.sum((tile_ids[:, None] >= tile_end[None, :]).astype(jnp.int32), axis=1)
    tile_expert = jnp.minimum(tile_expert, N_EXPERTS - 1)
    n_used = tile_end[-1:].astype(jnp.int32)
    dst = dst.astype(jnp.int32)
    tok = jnp.arange(t * TOPK_IN_GROUP, dtype=jnp.int32) // TOPK_IN_GROUP
    src_tok = jnp.zeros((n_tiles * tm,), jnp.int32).at[dst].set(tok)
    row_batch = (src_tok // seq).reshape(-1, 1)
    return dst, tile_expert.astype(jnp.int32), src_tok, n_used, row_batch


def _moe_layer(x, x_tm, scale, shift, gate, w_group, b_group, w_router, b_router, w1, w3, w2, layer,
               ln_g, ln_b, seq):
    route = _router(x, scale, shift, w_group, b_group, w_router, b_router, seq)
    dst, tile_expert, src_tok, n_used, row_batch = _sort_plan(route, seq)
    y_sorted = _experts(x_tm, scale, shift, tile_expert, src_tok, n_used, row_batch, w1, w3, w2,
                        layer)
    return _combine_ln(dst, y_sorted, route, x, gate, ln_g, ln_b, seq)


def _swa_layer(x, scale, shift, gate, tables, w_qkv, b_qkv, sinks, w_o, ln_g, ln_b, seq):
    qw = Q_HEADS * HEAD_DIM
    kvw = KV_HEADS * HEAD_DIM
    qkv = _mod_proj(x, scale, shift, w_qkv.astype(BF16), b_qkv, tables, seq,
                    rope_lo=qw // LANES, rope_hi=(qw + kvw) // LANES, tn=512)
    o = _band_attention(qkv, 0, qw // kvw, qw // kvw + 1, tables, seq, SWA_WINDOW, sinks=sinks)
    return _out_proj_ln([o], w_o.astype(BF16), x, gate, ln_g, ln_b, seq)


def _nsa_layer(x, scale, shift, gate, tables, w_in, pe_k, pe_v, phi_k1, phi_k2, phi_v1, phi_v2,
               w_o, ln_g, ln_b, seq):
    t, d = x.shape
    n_batch = t // seq
    qw = Q_HEADS * HEAD_DIM
    kvw = KV_HEADS * HEAD_DIM
    cols = [qw + i * kvw for i in range(7)]
    w_q, w_kc, w_vc, w_ks, w_vs, w_kw, w_vw, w_g = jnp.split(w_in, cols, axis=1)
    w_g = w_g.reshape(d, Q_HEADS, 3).transpose(0, 2, 1).reshape(d, 3 * Q_HEADS)
    w_g = jnp.pad(w_g, ((0, 0), (0, kvw - 3 * Q_HEADS)))
    w_all = jnp.concatenate([w_q, w_ks, w_kw, w_vs, w_vw, w_kc, w_vc, w_g], axis=1).astype(BF16)
    n_cols = w_all.shape[1]
    proj = _mod_proj(x, scale, shift, w_all, jnp.zeros((n_cols,), F32), tables, seq,
                     rope_lo=qw // LANES, rope_hi=(qw + 2 * kvw) // LANES, tn=768)
    kv_blk = qw // kvw
    k_s, v_s, k_c, v_c = (proj[:, qw + i * kvw: qw + (i + 1) * kvw] for i in (0, 2, 4, 5))
    gate_blk = (qw + 6 * kvw) // LANES

    kc = _compress(k_c, pe_k, phi_k1, phi_k2, seq)
    vc = _compress(v_c, pe_v, phi_v1, phi_v2, seq)
    o_cmp, selb = _cmp_attention(proj, kc, vc, gate_blk, seq, gate_col=0)

    heads = lambda a: a.reshape(n_batch, seq, KV_HEADS, HEAD_DIM).transpose(0, 2, 1, 3)
    onehot = (jnp.arange(seq)[:, None] // SEL_BLOCK == jnp.arange(HEAD_DIM)[None, :]).astype(BF16)
    onehot = jnp.broadcast_to(onehot, (n_batch, KV_HEADS, seq, HEAD_DIM))
    ks_h, vs_h = heads(k_s), heads(v_s)
    ka_even = jnp.concatenate([ks_h, onehot], axis=-1)
    ka_odd = jnp.concatenate([onehot, ks_h], axis=-1)
    vs2 = jnp.concatenate([vs_h, vs_h], axis=-1)
    o_sel = _sel_attention(proj, selb, ka_even, ka_odd, vs2, tables, gate_blk, seq, gate_col=Q_HEADS)
    o_win = _band_attention(proj, 0, kv_blk + 1, kv_blk + 3, tables, seq, NSA_WINDOW,
                            gate_blk=gate_blk, gate_col=2 * Q_HEADS)
    return _out_proj_ln([o_cmp, o_sel, o_win], w_o.astype(BF16), x, gate, ln_g, ln_b, seq)


def kernel(x, c, positions, w_ada, b_ada, swa_w_qkv, swa_b_qkv, swa_sinks, swa_w_o, nsa_w_in,
           nsa_pe_k, nsa_pe_v, nsa_phi_k1, nsa_phi_k2, nsa_phi_v1, nsa_phi_v2, nsa_w_o,
           moe_w_group, moe_b_group, moe_w_router, moe_b_router, moe_w1, moe_w3, moe_w2,
           ln_t_g, ln_t_b, ln_c_g, ln_c_b):
    n_batch, seq, d = x.shape
    depth = w_ada.shape[0]
    xt = x.reshape(n_batch * seq, d)
    mod = _adaln_mod(c, w_ada, b_ada)
    tables = _rope_tables(positions)
    for i in range(depth):
        sh_t, sc_t, g_t, sh_c, sc_c, g_c = (mod[i, :, k * d:(k + 1) * d] for k in range(6))
        j = i // 2
        if i % 2 == 0:
            xt, x_tm = _swa_layer(xt, sc_t, sh_t, g_t, tables, swa_w_qkv[j], swa_b_qkv[j], swa_sinks[j],
                            swa_w_o[j], ln_t_g[i], ln_t_b[i], seq)
        else:
            xt, x_tm = _nsa_layer(xt, sc_t, sh_t, g_t, tables, nsa_w_in[j], nsa_pe_k[j], nsa_pe_v[j],
                            nsa_phi_k1[j], nsa_phi_k2[j], nsa_phi_v1[j], nsa_phi_v2[j], nsa_w_o[j],
                            ln_t_g[i], ln_t_b[i], seq)
        xt = _moe_layer(xt, x_tm, sc_c, sh_c, g_c, moe_w_group[i], moe_b_group[i], moe_w_router[i],
                        moe_b_router[i], moe_w1, moe_w3, moe_w2, i, ln_c_g[i], ln_c_b[i], seq)
    return xt.reshape(n_batch, seq, d)
```

```python
import functools

import jax
import jax.numpy as jnp
from jax import lax
from jax.experimental import pallas as pl
from jax.experimental.pallas import tpu as pltpu

F32 = jnp.float32
BF16 = jnp.bfloat16

HEAD_DIM = 64
ROPE_DIM = HEAD_DIM // 4
ROPE_HALF = ROPE_DIM // 2
ROPE_THETA = 500000.0
Q_HEADS = 32
KV_HEADS = 4
GROUP = Q_HEADS // KV_HEADS
SWA_WINDOW = 128
NSA_WINDOW = 512
CMP_BLOCK = 32
CMP_STRIDE = 16
SEL_BLOCK = 64
SEL_TOPK = 16
FORCE_BONUS = 1e4
N_GROUPS = 4
EXPERTS_PER_GROUP = 4
N_EXPERTS = N_GROUPS * EXPERTS_PER_GROUP
TOPK_IN_GROUP = 2
DEPTH = 2
ALPHA = (2 * DEPTH) ** 0.25
LN_EPS = 1e-5
NEG_INF = -1e30
LOG2E = 1.4426950408889634

LANES = 128
ATTN_BLOCK = 128
VMEM_LIMIT = 56 * 1024 * 1024


def _cparams(*sem):
    return pltpu.CompilerParams(dimension_semantics=sem, vmem_limit_bytes=VMEM_LIMIT)


def _mod_kernel(cb_ref, w_ref, b_ref, o_ref, cs_ref, *, n_batch, tn):
    @pl.when((pl.program_id(0) == 0) & (pl.program_id(1) == 0))
    def _():
        c = cb_ref[...]
        cs_ref[...] = c * (1.0 / (1.0 + jnp.exp(-c)))

    for b in range(n_batch):
        cs = cs_ref[b]
        parts = []
        for g in range(tn // LANES):
            wg = w_ref[0, :, g * LANES:(g + 1) * LANES]
            parts.append(jnp.sum(wg * cs, axis=0, keepdims=True))
        o_ref[0, b:b + 1, :] = jnp.concatenate(parts, axis=-1) + b_ref[0]


def _adaln_mod(c, w_ada, b_ada):
    n_batch, d = c.shape
    depth, _, n = w_ada.shape
    tn = 512
    cb = jnp.broadcast_to(c[:, :, None], (n_batch, d, LANES))
    return pl.pallas_call(
        functools.partial(_mod_kernel, n_batch=n_batch, tn=tn),
        grid=(depth, n // tn),
        in_specs=[
            pl.BlockSpec((n_batch, d, LANES), lambda l, j: (0, 0, 0)),
            pl.BlockSpec((1, d, tn), lambda l, j: (l, 0, j)),
            pl.BlockSpec((1, 1, tn), lambda l, j: (l, 0, j)),
        ],
        out_specs=pl.BlockSpec((1, n_batch, tn), lambda l, j: (l, 0, j)),
        out_shape=jax.ShapeDtypeStruct((depth, n_batch, n), F32),
        scratch_shapes=[pltpu.VMEM((n_batch, d, LANES), F32)],
        compiler_params=_cparams("arbitrary", "arbitrary"),
    )(cb, w_ada, b_ada.reshape(depth, 1, n))


def _rope_table_kernel(pos_ref, inv_ref, c_ref, s1_ref, s2_ref):
    ang = pos_ref[...] * inv_ref[...]
    d = lax.broadcasted_iota(jnp.int32, ang.shape, 1) % HEAD_DIM
    cos = jnp.cos(ang)
    sin = jnp.sin(ang)
    c_ref[...] = jnp.where(d < ROPE_DIM, cos, 1.0)
    s1_ref[...] = jnp.where(d < ROPE_HALF, -sin, 0.0)
    s2_ref[...] = jnp.where((d >= ROPE_HALF) & (d < ROPE_DIM), sin, 0.0)


def _rope_tables(positions):
    t = positions.size
    tm = 1024
    inv_freq = ROPE_THETA ** (-jnp.arange(0, ROPE_DIM, 2, dtype=F32) / ROPE_DIM)
    lane = jnp.arange(LANES) % HEAD_DIM
    inv_lane = inv_freq[lane % ROPE_HALF].reshape(1, LANES)
    pos = positions.astype(F32).reshape(t, 1)
    spec = pl.BlockSpec((tm, LANES), lambda i: (i, 0))
    return pl.pallas_call(
        _rope_table_kernel,
        grid=(t // tm,),
        in_specs=[pl.BlockSpec((tm, 1), lambda i: (i, 0)),
                  pl.BlockSpec((1, LANES), lambda i: (0, 0))],
        out_specs=[spec, spec, spec],
        out_shape=[jax.ShapeDtypeStruct((t, LANES), F32)] * 3,
        compiler_params=_cparams("arbitrary"),
    )(pos, inv_lane)


def _apply_rope(x, c, s1, s2):
    reps = x.shape[-1] // LANES
    if reps > 1:
        c = jnp.concatenate([c] * reps, axis=-1)
        s1 = jnp.concatenate([s1] * reps, axis=-1)
        s2 = jnp.concatenate([s2] * reps, axis=-1)
    n = x.shape[-1]
    up = pltpu.roll(x, n - ROPE_HALF, 1)
    down = pltpu.roll(x, ROPE_HALF, 1)
    return x * c + up * s1 + down * s2


def _proj_kernel(x_ref, sc_ref, sh_ref, w_ref, b_ref, c_ref, s1_ref, s2_ref, o_ref, h_ref,
                 *, rope_lo, rope_hi, n_tiles):
    j = pl.program_id(1)
    groups = o_ref.shape[1] // LANES

    @pl.when(j == 0)
    def _():
        h_ref[...] = (x_ref[...] * (1.0 + sc_ref[0]) + sh_ref[0]).astype(BF16)

    acc = jnp.dot(h_ref[...], w_ref[...], preferred_element_type=F32) + b_ref[...]

    def write(pattern):
        for g in range(groups):
            piece = acc[:, g * LANES:(g + 1) * LANES]
            if pattern[g]:
                piece = _apply_rope(piece, c_ref[...], s1_ref[...], s2_ref[...])
            o_ref[:, g * LANES:(g + 1) * LANES] = piece.astype(o_ref.dtype)

    patterns = {}
    for tile in range(n_tiles):
        pattern = tuple(rope_lo <= tile * groups + g < rope_hi for g in range(groups))
        patterns.setdefault(pattern, []).append(tile)
    for pattern, tiles in patterns.items():
        cond = j == tiles[0]
        for tile in tiles[1:]:
            cond = cond | (j == tile)
        pl.when(cond)(functools.partial(write, pattern))


def _mod_proj(x, scale, shift, w, bias, tables, seq, *, rope_lo=0, rope_hi=0, tn=256, tm=1024,
              out_dtype=BF16):
    t, d = x.shape
    n = w.shape[1]
    n_batch = scale.shape[0]
    tm = min(tm, seq)
    per_batch = seq // tm
    c_tab, s1_tab, s2_tab = tables
    vec = pl.BlockSpec((1, 1, d), lambda i, j: (i // per_batch, 0, 0))
    tab = pl.BlockSpec((tm, LANES), lambda i, j: (i, 0))
    return pl.pallas_call(
        functools.partial(_proj_kernel, rope_lo=rope_lo, rope_hi=rope_hi, n_tiles=n // tn),
        grid=(t // tm, n // tn),
        in_specs=[
            pl.BlockSpec((tm, d), lambda i, j: (i, 0)),
            vec, vec,
            pl.BlockSpec((d, tn), lambda i, j: (0, j)),
            pl.BlockSpec((1, tn), lambda i, j: (0, j)),
            tab, tab, tab,
        ],
        out_specs=pl.BlockSpec((tm, tn), lambda i, j: (i, j)),
        out_shape=jax.ShapeDtypeStruct((t, n), out_dtype),
        scratch_shapes=[pltpu.VMEM((tm, d), BF16)],
        compiler_params=_cparams("arbitrary", "arbitrary"),
    )(x, scale.reshape(n_batch, 1, d), shift.reshape(n_batch, 1, d), w, bias.reshape(1, n),
      c_tab, s1_tab, s2_tab)


def _scaled_q(q_ref, tables=None):
    q = q_ref[...].astype(F32)
    if tables is not None:
        q = _apply_rope(q, *(t[...] for t in tables))
    return q * (HEAD_DIM ** -0.5 * LOG2E)


def _split_even_odd(q):
    even = (lax.broadcasted_iota(jnp.int32, q.shape, 1) % LANES) < HEAD_DIM
    return jnp.where(even, q, 0.0).astype(BF16), jnp.where(even, 0.0, q).astype(BF16)


def _stack_heads(q_even, q_odd, first_head):
    parts = []
    for g in range(GROUP):
        head = first_head + g
        src = q_even if head % 2 == 0 else q_odd
        parts.append(src[:, (head // 2) * LANES:(head // 2 + 1) * LANES])
    return jnp.concatenate(parts, axis=0)


def _twice(x):
    return jnp.concatenate([x, x], axis=-1)


def _merge_pairs(o, rows, scale):
    even = lax.broadcasted_iota(jnp.int32, (rows, LANES), 1) < HEAD_DIM
    out = []
    for g in range(0, GROUP, 2):
        a = o[g * rows:(g + 1) * rows] * scale[g]
        b = o[(g + 1) * rows:(g + 2) * rows] * scale[g + 1]
        out.append(jnp.where(even, a, b))
    return jnp.concatenate(out, axis=-1)


def _sigmoid(x):
    return 1.0 / (1.0 + jnp.exp(-x))


def _band_attn_kernel(*refs, window, use_sinks, use_gate, gate_col):
    it = iter(refs)
    q_ref, k_ref, v_ref, c_ref, s1_ref, s2_ref = (next(it) for _ in range(6))
    sink_ref = next(it) if use_sinks else None
    gate_ref = next(it) if use_gate else None
    o_ref = next(it)

    n = pl.program_id(1)
    rows = ATTN_BLOCK
    halo = -(-window // rows) * rows
    span = rows + halo
    start = jnp.maximum(n * rows - halo, 0)
    start = pl.multiple_of(start, rows)
    q_even, q_odd = _split_even_odd(_scaled_q(q_ref, (c_ref, s1_ref, s2_ref)))
    k = k_ref[pl.ds(start, span), :]
    v = v_ref[pl.ds(start, span), :]
    qpos = n * rows + lax.broadcasted_iota(jnp.int32, (rows, span), 0)
    kpos = start + lax.broadcasted_iota(jnp.int32, (rows, span), 1)
    rel = qpos - kpos
    mask = (rel >= 0) & (rel < window)
    if use_gate:
        gate = _sigmoid(gate_ref[...].astype(F32))

    for j in range(KV_HEADS):
        k2 = _twice(k[:, j * HEAD_DIM:(j + 1) * HEAD_DIM])
        v2 = _twice(v[:, j * HEAD_DIM:(j + 1) * HEAD_DIM])
        q8 = _stack_heads(q_even, q_odd, j * GROUP)
        s = lax.dot_general(q8, k2, (((1,), (1,)), ((), ())), preferred_element_type=F32)
        ps, scale = [], []
        for g in range(GROUP):
            sg = jnp.where(mask, s[g * rows:(g + 1) * rows], NEG_INF)
            m = jnp.max(sg, axis=-1, keepdims=True)
            if use_sinks:
                sink = sink_ref[j * GROUP + g] * LOG2E
                m = jnp.maximum(m, sink)
            e = jnp.exp2(sg - m)
            den = jnp.sum(e, axis=-1, keepdims=True)
            if use_sinks:
                den = den + jnp.exp2(sink - m)
            inv = 1.0 / den
            if use_gate:
                col = gate_col + j * GROUP + g
                inv = inv * gate[:, col:col + 1]
            ps.append(e.astype(BF16))
            scale.append(inv)
        o = jnp.dot(jnp.concatenate(ps, axis=0), v2, preferred_element_type=F32)
        o_ref[:, j * GROUP * HEAD_DIM:(j + 1) * GROUP * HEAD_DIM] = (
            _merge_pairs(o, rows, scale).astype(o_ref.dtype))


def _band_attention(proj, q_blk, k_blk, v_blk, tables, seq, window, sinks=None, gate_blk=None,
                    gate_col=0):
    t = proj.shape[0]
    n_batch = t // seq
    nb = seq // ATTN_BLOCK
    qw = Q_HEADS * HEAD_DIM
    kvw = KV_HEADS * HEAD_DIM
    row = lambda w, c: pl.BlockSpec((ATTN_BLOCK, w), lambda b, i: (b * nb + i, c))
    res = lambda c: pl.BlockSpec((seq, kvw), lambda b, i: (b, c))
    in_specs = [row(qw, q_blk), res(k_blk), res(v_blk), row(LANES, 0), row(LANES, 0), row(LANES, 0)]
    args = [proj, proj, proj, *tables]
    if sinks is not None:
        in_specs.append(pl.BlockSpec(memory_space=pltpu.SMEM))
        args.append(sinks)
    if gate_blk is not None:
        in_specs.append(row(LANES, gate_blk))
        args.append(proj)
    return pl.pallas_call(
        functools.partial(_band_attn_kernel, window=window, use_sinks=sinks is not None,
                          use_gate=gate_blk is not None, gate_col=gate_col),
        grid=(n_batch, nb),
        in_specs=in_specs,
        out_specs=row(qw, 0),
        out_shape=jax.ShapeDtypeStruct((t, qw), BF16),
        compiler_params=_cparams("arbitrary", "arbitrary"),
    )(*args)


def _layer_norm(y, g, b):
    mu = jnp.mean(y, axis=-1, keepdims=True)
    yc = y - mu
    var = jnp.mean(yc * yc, axis=-1, keepdims=True)
    return yc * lax.rsqrt(var + LN_EPS) * g + b


def _out_proj_kernel(*refs, n_parts):
    o_parts = refs[:n_parts]
    w_ref, x_ref, gate_ref, g_ref, b_ref, out_ref, out_tm_ref = refs[n_parts:]
    if n_parts == 1:
        o = o_parts[0][...]
    else:
        acc = o_parts[0][...].astype(F32)
        for r in o_parts[1:]:
            acc = acc + r[...].astype(F32)
        o = acc.astype(BF16)
    mix = jnp.dot(o, w_ref[...], preferred_element_type=F32)
    y = ALPHA * x_ref[...] + gate_ref[0] * mix
    res = _layer_norm(y, g_ref[...], b_ref[...])
    out_ref[...] = res
    _store_token_major(out_tm_ref, res)


def _out_proj_ln(o_parts, w_o, x, gate, ln_g, ln_b, seq, tm=256):
    t, d = x.shape
    n = d // LANES
    n_batch = gate.shape[0]
    per_batch = seq // tm
    k = w_o.shape[0]
    row_o = pl.BlockSpec((tm, k), lambda i: (i, 0))
    row_x = pl.BlockSpec((tm, d), lambda i: (i, 0))
    vec = pl.BlockSpec((1, d), lambda i: (0, 0))
    return pl.pallas_call(
        functools.partial(_out_proj_kernel, n_parts=len(o_parts)),
        grid=(t // tm,),
        in_specs=[row_o] * len(o_parts) + [
            pl.BlockSpec((k, d), lambda i: (0, 0)),
            row_x,
            pl.BlockSpec((1, 1, d), lambda i: (i // per_batch, 0, 0)),
            vec, vec],
        out_specs=[row_x, pl.BlockSpec((tm * n, LANES), lambda i: (i, 0))],
        out_shape=[jax.ShapeDtypeStruct((t, d), F32), jax.ShapeDtypeStruct((t * n, LANES), F32)],
        compiler_params=_cparams("arbitrary"),
    )(*o_parts, w_o, x, gate.reshape(n_batch, 1, d), ln_g.reshape(1, d), ln_b.reshape(1, d))


def _compress_kernel(x_ref, pe_ref, w1_ref, w1a_ref, w1b_ref, w2_ref, o_ref):
    x = x_ref[0]
    a = jnp.dot(x, w1a_ref[...], preferred_element_type=F32)
    b = jnp.dot(x, w1b_ref[...], preferred_element_type=F32)
    nc = a.shape[0]
    b_next = pltpu.roll(b, nc - 1, 0)
    pe_term = jnp.dot(pe_ref[...], w1_ref[...], preferred_element_type=F32)[0:1]
    hid = a + b_next + jnp.concatenate([pe_term] * KV_HEADS, axis=-1)
    hid = hid * (1.0 / (1.0 + jnp.exp(-hid)))
    o_ref[0] = jnp.dot(hid.astype(BF16), w2_ref[...], preferred_element_type=F32).astype(o_ref.dtype)


def _block_diag_heads(w):
    p, d, n = w.shape
    eye = jnp.eye(KV_HEADS, dtype=w.dtype)
    big = w[:, None, :, None, :] * eye[None, :, None, :, None]
    return big.reshape(p * KV_HEADS * d, KV_HEADS * n)


def _compress(xc, pe, w1, w2, seq):
    t, kvw = xc.shape
    n_batch = t // seq
    nch = seq // CMP_STRIDE
    hidden = w1.shape[1]
    x = xc.reshape(n_batch, nch, CMP_STRIDE * kvw)
    w1r = w1.reshape(CMP_BLOCK, HEAD_DIM, hidden)
    w1a = _block_diag_heads(w1r[:CMP_STRIDE]).astype(BF16)
    w1b = _block_diag_heads(w1r[CMP_STRIDE:]).astype(BF16)
    eye = jnp.eye(KV_HEADS, dtype=w2.dtype)
    w2d = (w2[None, :, None, :] * eye[:, None, :, None]).reshape(KV_HEADS * hidden, kvw).astype(BF16)
    pe8 = jnp.broadcast_to(pe.reshape(1, CMP_BLOCK * HEAD_DIM), (8, CMP_BLOCK * HEAD_DIM)).astype(BF16)
    full = lambda a: pl.BlockSpec(a.shape, lambda b: (0,) * a.ndim)
    w1b16 = w1.astype(BF16)
    return pl.pallas_call(
        _compress_kernel,
        grid=(n_batch,),
        in_specs=[pl.BlockSpec((1, nch, CMP_STRIDE * kvw), lambda b: (b, 0, 0)),
                  full(pe8), full(w1b16), full(w1a), full(w1b), full(w2d)],
        out_specs=pl.BlockSpec((1, nch, kvw), lambda b: (b, 0, 0)),
        out_shape=jax.ShapeDtypeStruct((n_batch, nch, kvw), BF16),
        compiler_params=_cparams("arbitrary"),
    )(x, pe8, w1b16, w1a, w1b, w2d)


def _topk_bias(score, top_k):
    nblk, rows = score.shape
    sub = lax.broadcasted_iota(jnp.int32, (8, rows), 0)
    chunks = [score[c * 8:(c + 1) * 8] for c in range(nblk // 8)]
    counts = [jnp.zeros((8, rows), F32) for _ in chunks]
    for sp in range(nblk):
        row = score[sp:sp + 1]
        for c, chunk in enumerate(chunks):
            ge = jnp.where(row >= chunk, 1.0, 0.0)
            gt = jnp.where(row > chunk, 1.0, 0.0)
            if sp < c * 8:
                beats = ge
            elif sp >= (c + 1) * 8:
                beats = gt
            else:
                beats = jnp.where(sub > sp - c * 8, ge, gt)
            counts[c] = counts[c] + beats
    return jnp.concatenate([jnp.where(cnt < top_k, 0.0, NEG_INF) for cnt in counts], axis=0)


def _cmp_attn_kernel(q_ref, kc_ref, vc_ref, ovt_ref, gate_ref, o_ref, sel_ref, *, gate_col, top_k):
    n = pl.program_id(1)
    rows = ATTN_BLOCK
    nc = kc_ref.shape[1]
    nblk = ovt_ref.shape[0]
    q_even, q_odd = _split_even_odd(_scaled_q(q_ref))
    kc = kc_ref[0]
    vc = vc_ref[0]
    t = n * rows + lax.broadcasted_iota(jnp.int32, (rows, nc), 0)
    cmp_end = lax.broadcasted_iota(jnp.int32, (rows, nc), 1) * CMP_STRIDE + (CMP_BLOCK - 1)
    valid = cmp_end <= t
    has_valid = n * rows + lax.broadcasted_iota(jnp.int32, (rows, 1), 0) >= CMP_BLOCK - 1
    gate = _sigmoid(gate_ref[...].astype(F32))

    ts = n * rows + lax.broadcasted_iota(jnp.int32, (nblk, rows), 1)
    blk = lax.broadcasted_iota(jnp.int32, (nblk, rows), 0)
    cur = lax.shift_right_arithmetic(ts, SEL_BLOCK.bit_length() - 1)
    causal = blk * SEL_BLOCK <= ts
    forced = (blk == 0) | (blk == cur) | (blk == cur - 1)
    bonus = jnp.where(forced, FORCE_BONUS, 0.0)

    for j in range(KV_HEADS):
        k2 = _twice(kc[:, j * HEAD_DIM:(j + 1) * HEAD_DIM])
        v2 = _twice(vc[:, j * HEAD_DIM:(j + 1) * HEAD_DIM])
        q8 = _stack_heads(q_even, q_odd, j * GROUP)
        s = lax.dot_general(q8, k2, (((1,), (1,)), ((), ())), preferred_element_type=F32)
        ps, scale, psum = [], [], None
        for g in range(GROUP):
            sg = jnp.where(valid, s[g * rows:(g + 1) * rows], NEG_INF)
            m = jnp.max(sg, axis=-1, keepdims=True)
            e = jnp.exp2(sg - m)
            den = jnp.sum(e, axis=-1, keepdims=True)
            inv = jnp.where(has_valid, 1.0 / den, 0.0)
            pn = e * inv
            psum = pn if psum is None else psum + pn
            col = gate_col + j * GROUP + g
            ps.append(e.astype(BF16))
            scale.append(inv * gate[:, col:col + 1])
        o = jnp.dot(jnp.concatenate(ps, axis=0), v2, preferred_element_type=F32)
        o_ref[:, j * GROUP * HEAD_DIM:(j + 1) * GROUP * HEAD_DIM] = (
            _merge_pairs(o, rows, scale).astype(o_ref.dtype))

        imp_t = lax.dot_general(ovt_ref[...], psum, (((1,), (1,)), ((), ())),
                                preferred_element_type=F32,
                                precision=lax.Precision.HIGHEST)
        score = jnp.where(causal, imp_t + bonus, NEG_INF)
        bias_t = _topk_bias(score, top_k)
        sel_ref[0, j] = jnp.concatenate([bias_t, bias_t], axis=0).T.astype(sel_ref.dtype)


def _cmp_attention(proj, kc, vc, gate_blk, seq, gate_col):
    t = proj.shape[0]
    n_batch = t // seq
    nb = seq // ATTN_BLOCK
    nc = kc.shape[1]
    nsel = seq // SEL_BLOCK
    assert nsel <= HEAD_DIM
    qw = Q_HEADS * HEAD_DIM
    top_k = min(SEL_TOPK, nsel)
    c0 = jnp.arange(nc)[None, :] * CMP_STRIDE
    s0 = jnp.arange(HEAD_DIM)[:, None] * SEL_BLOCK
    ov = jnp.clip(jnp.minimum(c0 + CMP_BLOCK, s0 + SEL_BLOCK) - jnp.maximum(c0, s0), 0)
    overlap_t = ov.astype(F32) / CMP_BLOCK
    row = lambda w, c: pl.BlockSpec((ATTN_BLOCK, w), lambda b, i: (b * nb + i, c))
    res = pl.BlockSpec((1, nc, kc.shape[2]), lambda b, i: (b, 0, 0))
    return pl.pallas_call(
        functools.partial(_cmp_attn_kernel, gate_col=gate_col, top_k=top_k),
        grid=(n_batch, nb),
        in_specs=[row(qw, 0), res, res,
                  pl.BlockSpec((HEAD_DIM, nc), lambda b, i: (0, 0)), row(LANES, gate_blk)],
        out_specs=[row(qw, 0),
                   pl.BlockSpec((1, KV_HEADS, ATTN_BLOCK, LANES), lambda b, i: (b, 0, i, 0))],
        out_shape=[jax.ShapeDtypeStruct((t, qw), BF16),
                   jax.ShapeDtypeStruct((n_batch, KV_HEADS, seq, LANES), BF16)],
        compiler_params=_cparams("arbitrary", "arbitrary"),
    )(proj, kc, vc, overlap_t, proj)


SEL_TILE = 512


def _sel_attn_kernel(q_ref, sel_ref, kae_ref, kao_ref, v_ref, c_ref, s1_ref, s2_ref, gate_ref, o_ref,
                     m_ref, l_ref, acc_ref, *, gate_col):
    j = pl.program_id(1)
    n = pl.program_id(2)
    rows = ATTN_BLOCK
    tk = SEL_TILE
    pairs = GROUP // 2
    q = _scaled_q(q_ref, (c_ref, s1_ref, s2_ref))
    bias2 = sel_ref[0, 0].astype(F32)
    low = lax.broadcasted_iota(jnp.int32, (rows, LANES), 1) < HEAD_DIM
    lhs = []
    for parity in range(2):
        parts = []
        for i in range(pairs):
            slab = q[:, i * LANES:(i + 1) * LANES]
            parts.append(jnp.where(low, slab, bias2) if parity == 0 else jnp.where(low, bias2, slab))
        lhs.append(jnp.concatenate(parts, axis=0).astype(BF16))
    ka_refs = (kae_ref, kao_ref)

    m_ref[...] = jnp.full(m_ref.shape, NEG_INF, F32)
    l_ref[...] = jnp.zeros(l_ref.shape, F32)
    acc_ref[...] = jnp.zeros(acc_ref.shape, F32)

    def tile(kt, mask_bias):
        off = pl.multiple_of(kt * tk, tk)
        v2 = v_ref[0, 0, pl.ds(off, tk), :]
        scores = []
        for parity in range(2):
            ka = ka_refs[parity][0, 0, pl.ds(off, tk), :]
            s = lax.dot_general(lhs[parity], ka, (((1,), (1,)), ((), ())),
                                preferred_element_type=F32)
            scores.append(s if mask_bias is None else s + mask_bias)
        probs, alphas = [], []
        for parity in range(2):
            s = scores[parity]
            chunks = [s[:, c * LANES:(c + 1) * LANES] for c in range(tk // LANES)]
            rmax = chunks[0]
            for ch in chunks[1:]:
                rmax = jnp.maximum(rmax, ch)
            m_old = m_ref[parity]
            m_new = jnp.maximum(m_old, jnp.max(rmax, axis=-1, keepdims=True))
            alpha = jnp.exp2(m_old - m_new)
            ps = [jnp.exp2(ch - m_new) for ch in chunks]
            lsum = ps[0]
            for x in ps[1:]:
                lsum = lsum + x
            l_ref[parity] = l_ref[parity] * alpha + lsum
            m_ref[parity] = m_new
            probs.append(jnp.concatenate([x.astype(BF16) for x in ps], axis=-1))
            alphas.append(alpha)
        for parity in range(2):
            acc_ref[parity] = acc_ref[parity] * alphas[parity] + jnp.dot(
                probs[parity], v2, preferred_element_type=F32)

    def body(kt, carry):
        tile(kt, None)
        return carry

    n_full = (n * rows) // tk
    lax.fori_loop(0, n_full, body, 0)
    qpos = n * rows + lax.broadcasted_iota(jnp.int32, (rows, tk), 0)
    kpos = n_full * tk + lax.broadcasted_iota(jnp.int32, (rows, tk), 1)
    causal = jnp.where(kpos <= qpos, 0.0, NEG_INF)
    tile(n_full, jnp.concatenate([causal] * pairs, axis=0))

    gate = _sigmoid(gate_ref[...].astype(F32))
    lane = lax.broadcasted_iota(jnp.int32, gate.shape, 1)
    outs = []
    for i in range(pairs):
        scaled = []
        for parity in range(2):
            col = gate_col + j * GROUP + 2 * i + parity
            gcol = jnp.sum(jnp.where(lane == col, gate, 0.0), axis=-1, keepdims=True)
            l = jnp.sum(l_ref[parity, i * rows:(i + 1) * rows], axis=-1, keepdims=True)
            scaled.append(acc_ref[parity, i * rows:(i + 1) * rows] * (gcol / l))
        outs.append(jnp.where(low, scaled[0], scaled[1]))
    o_ref[...] = jnp.concatenate(outs, axis=-1).astype(o_ref.dtype)


def _sel_attention(proj, selb, ka_even, ka_odd, v2, tables, gate_blk, seq, gate_col):
    t = proj.shape[0]
    n_batch = t // seq
    nb = seq // ATTN_BLOCK
    gw = GROUP * HEAD_DIM
    pairs = GROUP // 2
    row = lambda c: pl.BlockSpec((ATTN_BLOCK, LANES), lambda b, j, i: (b * nb + i, c))
    res = pl.BlockSpec((1, 1, seq, LANES), lambda b, j, i: (b, j, 0, 0))
    acc = pltpu.VMEM((2, pairs * ATTN_BLOCK, LANES), F32)
    return pl.pallas_call(
        functools.partial(_sel_attn_kernel, gate_col=gate_col),
        grid=(n_batch, KV_HEADS, nb),
        in_specs=[
            pl.BlockSpec((ATTN_BLOCK, gw), lambda b, j, i: (b * nb + i, j)),
            pl.BlockSpec((1, 1, ATTN_BLOCK, LANES), lambda b, j, i: (b, j, i, 0)),
            res, res, res, row(0), row(0), row(0), row(gate_blk)],
        out_specs=pl.BlockSpec((ATTN_BLOCK, gw), lambda b, j, i: (b * nb + i, j)),
        out_shape=jax.ShapeDtypeStruct((t, Q_HEADS * HEAD_DIM), BF16),
        scratch_shapes=[acc, acc, acc],
        compiler_params=_cparams("arbitrary", "arbitrary", "arbitrary"),
    )(proj, selb, ka_even, ka_odd, v2, *tables, proj)


def _router_kernel(x_ref, sc_ref, sh_ref, w_ref, b_ref, o_ref):
    h = x_ref[...] * (1.0 + sc_ref[0]) + sh_ref[0]
    w = w_ref[...]
    h_hi = h.astype(BF16)
    h_lo = (h - h_hi.astype(F32)).astype(BF16)
    w_hi = w.astype(BF16)
    w_lo = (w - w_hi.astype(F32)).astype(BF16)
    logits = (jnp.dot(h_hi, w_hi, preferred_element_type=F32)
              + jnp.dot(h_lo, w_hi, preferred_element_type=F32)
              + jnp.dot(h_hi, w_lo, preferred_element_type=F32)) + b_ref[...]
    lane = lax.broadcasted_iota(jnp.int32, logits.shape, 1)
    big = 1 << 20

    gmask = lane < N_GROUPS
    lg = jnp.where(gmask, logits, -jnp.inf)
    eg = jnp.exp(lg - jnp.max(lg, axis=-1, keepdims=True))
    pg = eg / jnp.sum(eg, axis=-1, keepdims=True)
    g_prob = jnp.max(pg, axis=-1, keepdims=True)
    g_idx = jnp.min(jnp.where((pg == g_prob) & gmask, lane, big), axis=-1, keepdims=True)

    lo = N_GROUPS + g_idx * EXPERTS_PER_GROUP
    emask = (lane >= lo) & (lane < lo + EXPERTS_PER_GROUP)
    le = jnp.where(emask, logits, -jnp.inf)
    ee = jnp.exp(le - jnp.max(le, axis=-1, keepdims=True))
    pe = jnp.where(emask, ee / jnp.sum(ee, axis=-1, keepdims=True), -1.0)
    p1 = jnp.max(pe, axis=-1, keepdims=True)
    i1 = jnp.min(jnp.where(pe == p1, lane, big), axis=-1, keepdims=True)
    pe2 = jnp.where(lane == i1, -1.0, pe)
    p2 = jnp.max(pe2, axis=-1, keepdims=True)
    i2 = jnp.min(jnp.where(pe2 == p2, lane, big), axis=-1, keepdims=True)
    tot = p1 + p2
    w1 = g_prob * (p1 / tot)
    w2 = g_prob * (p2 / tot)
    e1 = (i1 - N_GROUPS).astype(F32)
    e2 = (i2 - N_GROUPS).astype(F32)
    o_ref[...] = jnp.where(lane == 0, e1, jnp.where(lane == 1, e2,
                           jnp.where(lane == 2, w1, jnp.where(lane == 3, w2, 0.0))))


def _router(x, scale, shift, w_group, b_group, w_router, b_router, seq, tm=512):
    t, d = x.shape
    n_batch = scale.shape[0]
    per_batch = seq // tm
    w = jnp.zeros((d, LANES), F32).at[:, :N_GROUPS].set(w_group)
    w = w.at[:, N_GROUPS:N_GROUPS + N_EXPERTS].set(w_router)
    b = jnp.zeros((1, LANES), F32).at[0, :N_GROUPS].set(b_group)
    b = b.at[0, N_GROUPS:N_GROUPS + N_EXPERTS].set(b_router)
    vec = pl.BlockSpec((1, 1, d), lambda i: (i // per_batch, 0, 0))
    return pl.pallas_call(
        _router_kernel,
        grid=(t // tm,),
        in_specs=[pl.BlockSpec((tm, d), lambda i: (i, 0)), vec, vec,
                  pl.BlockSpec((d, LANES), lambda i: (0, 0)),
                  pl.BlockSpec((1, LANES), lambda i: (0, 0))],
        out_specs=pl.BlockSpec((tm, LANES), lambda i: (i, 0)),
        out_shape=jax.ShapeDtypeStruct((t, LANES), F32),
        compiler_params=_cparams("arbitrary"),
    )(x, scale.reshape(n_batch, 1, d), shift.reshape(n_batch, 1, d), w, b)


MOE_TILE = 256


def _store_token_major(ref, val):
    rows, n = val.shape[0], val.shape[1] // LANES
    for a in range(n):
        ref[pl.ds(a, rows, stride=n), :] = val[:, a * LANES:(a + 1) * LANES]


def _load_token_major(ref, rows, n):
    return jnp.concatenate([ref[pl.ds(a, rows, stride=n), :] for a in range(n)], axis=-1)


def _expert_kernel(te_ref, src_ref, nt_ref, x_hbm, rowb_ref, sc_ref, sh_ref, w1_ref, w3_ref, w2_ref,
                   o_ref, xbuf, w1b, w3b, w2b, sem, *, n_batch):
    i = pl.program_id(0)
    n_used = nt_ref[0]
    tm = MOE_TILE
    n = sc_ref.shape[-1] // LANES

    def row_copy(tile_idx, slot, r):
        tok = src_ref[tile_idx * tm + r]
        return pltpu.make_async_copy(x_hbm.at[pl.ds(pl.multiple_of(tok * n, n), n), :],
                                     xbuf.at[slot, pl.ds(pl.multiple_of(r * n, n), n), :],
                                     sem.at[slot])

    def start_rows(tile_idx, slot, lo, hi):
        def body(r, c):
            row_copy(tile_idx, slot, r).start()
            return c
        lax.fori_loop(lo, hi, body, 0, unroll=8)

    def wait_tile(slot):
        pltpu.make_async_copy(x_hbm.at[pl.ds(0, tm * n), :], xbuf.at[slot], sem.at[slot]).wait()

    @pl.when((i == 0) & (n_used > 0))
    def _():
        start_rows(0, 0, 0, tm)

    new_expert = (i == 0) | (te_ref[i] != te_ref[jnp.maximum(i - 1, 0)])

    @pl.when((i < n_used) & new_expert)
    def _():
        w1b[...] = w1_ref[0, 0].astype(BF16)
        w3b[...] = w3_ref[0, 0].astype(BF16)
        w2b[...] = w2_ref[0, 0].astype(BF16)

    @pl.when(i < n_used)
    def _():
        slot = i % 2
        wait_tile(slot)
        x = _load_token_major(xbuf.at[slot], tm, n)
        rowb = rowb_ref[...]
        h = x * (1.0 + sc_ref[0]) + sh_ref[0]
        for b in range(1, n_batch):
            h = jnp.where(rowb == b, x * (1.0 + sc_ref[b]) + sh_ref[b], h)
        hb = h.astype(BF16)

        nxt = jnp.minimum(i + 1, n_used - 1)
        nslot = (i + 1) % 2
        quarter = tm // 4

        def prefetch(k):
            for r in range(k * quarter, (k + 1) * quarter):
                row_copy(nxt, nslot, r).start()

        prefetch(0)
        a = jnp.dot(hb, w1b[...], preferred_element_type=F32)
        prefetch(1)
        g = jnp.dot(hb, w3b[...], preferred_element_type=F32)
        prefetch(2)
        he = (a * _sigmoid(a) * g).astype(BF16)
        y = jnp.dot(he, w2b[...], preferred_element_type=F32)
        prefetch(3)
        _store_token_major(o_ref, y)

        @pl.when(i + 1 >= n_used)
        def _():
            wait_tile(nslot)

    @pl.when(i >= n_used)
    def _():
        o_ref[...] = jnp.zeros(o_ref.shape, o_ref.dtype)


def _experts(x_tm, scale, shift, tile_expert, src_tok, n_used, row_batch, w1, w3, w2, layer):
    n_batch, d = scale.shape
    n = d // LANES
    n_tiles = tile_expert.shape[0]
    de = w1.shape[3]
    tm = MOE_TILE
    grid_spec = pltpu.PrefetchScalarGridSpec(
        num_scalar_prefetch=3,
        grid=(n_tiles,),
        in_specs=[
            pl.BlockSpec(memory_space=pl.ANY),
            pl.BlockSpec((tm, 1), lambda i, te, src, nt: (i, 0)),
            pl.BlockSpec((n_batch, 1, d), lambda i, te, src, nt: (0, 0, 0)),
            pl.BlockSpec((n_batch, 1, d), lambda i, te, src, nt: (0, 0, 0)),
            pl.BlockSpec((1, 1, d, de), lambda i, te, src, nt: (layer, te[i], 0, 0)),
            pl.BlockSpec((1, 1, d, de), lambda i, te, src, nt: (layer, te[i], 0, 0)),
            pl.BlockSpec((1, 1, de, d), lambda i, te, src, nt: (layer, te[i], 0, 0)),
        ],
        out_specs=pl.BlockSpec((tm * n, LANES), lambda i, te, src, nt: (i, 0)),
        scratch_shapes=[pltpu.VMEM((2, tm * n, LANES), F32),
                        pltpu.VMEM((d, de), BF16), pltpu.VMEM((d, de), BF16),
                        pltpu.VMEM((de, d), BF16), pltpu.SemaphoreType.DMA((2,))],
    )
    return pl.pallas_call(
        functools.partial(_expert_kernel, n_batch=n_batch),
        grid_spec=grid_spec,
        out_shape=jax.ShapeDtypeStruct((n_tiles * tm * n, LANES), F32),
        compiler_params=_cparams("arbitrary"),
    )(tile_expert, src_tok, n_used, x_tm, row_batch, scale.reshape(n_batch, 1, d),
      shift.reshape(n_batch, 1, d), w1, w3, w2)


COMBINE_TILE = 256


def _combine_kernel(dst_ref, y_hbm, route_ref, x_ref, gate_ref, g_ref, b_ref, out_ref, ybuf, sem):
    i = pl.program_id(0)
    n_steps = pl.num_programs(0)
    tm = COMBINE_TILE

    n = x_ref.shape[-1] // LANES

    def row_copy(step, slot, k, r):
        pos = dst_ref[(step * tm + r) * TOPK_IN_GROUP + k]
        return pltpu.make_async_copy(
            y_hbm.at[pl.ds(pl.multiple_of(pos * n, n), n), :],
            ybuf.at[slot, pl.ds(pl.multiple_of((k * tm + r) * n, n), n), :], sem.at[slot])

    def start_step(step, slot):
        def body(r, c):
            for k in range(TOPK_IN_GROUP):
                row_copy(step, slot, k, r).start()
            return c
        lax.fori_loop(0, tm, body, 0, unroll=4)

    def wait_step(slot):
        pltpu.make_async_copy(y_hbm.at[pl.ds(0, TOPK_IN_GROUP * tm * n), :], ybuf.at[slot],
                              sem.at[slot]).wait()

    @pl.when(i == 0)
    def _():
        start_step(0, 0)

    @pl.when(i + 1 < n_steps)
    def _():
        start_step(i + 1, (i + 1) % 2)

    slot = i % 2
    wait_step(slot)
    route = route_ref[...]
    lane = lax.broadcasted_iota(jnp.int32, route.shape, 1)
    mix = None
    for k in range(TOPK_IN_GROUP):
        wk = jnp.sum(jnp.where(lane == TOPK_IN_GROUP + k, route, 0.0), axis=-1, keepdims=True)
        term = wk * _load_token_major(ybuf.at[slot, pl.ds(k * tm * n, tm * n), :], tm, n)
        mix = term if mix is None else mix + term
    y = ALPHA * x_ref[...] + gate_ref[0] * mix
    out_ref[...] = _layer_norm(y, g_ref[...], b_ref[...])


def _combine_ln(dst, y_sorted, route, x, gate, ln_g, ln_b, seq):
    t, d = x.shape
    n_batch = gate.shape[0]
    tm = COMBINE_TILE
    per_batch = seq // tm
    row = lambda w: pl.BlockSpec((tm, w), lambda i, dst: (i, 0))
    vec = pl.BlockSpec((1, d), lambda i, dst: (0, 0))
    grid_spec = pltpu.PrefetchScalarGridSpec(
        num_scalar_prefetch=1,
        grid=(t // tm,),
        in_specs=[pl.BlockSpec(memory_space=pl.ANY), row(LANES), row(d),
                  pl.BlockSpec((1, 1, d), lambda i, dst: (i // per_batch, 0, 0)), vec, vec],
        out_specs=row(d),
        scratch_shapes=[pltpu.VMEM((2, TOPK_IN_GROUP * tm * (d // LANES), LANES), F32),
                        pltpu.SemaphoreType.DMA((2,))],
    )
    return pl.pallas_call(
        _combine_kernel,
        grid_spec=grid_spec,
        out_shape=jax.ShapeDtypeStruct((t, d), F32),
        compiler_params=_cparams("arbitrary"),
    )(dst, y_sorted, route, x, gate.reshape(n_batch, 1, d), ln_g.reshape(1, d), ln_b.reshape(1, d))


def _sort_plan(route, seq):
    t = route.shape[0]
    tm = MOE_TILE
    eid = route[:, :TOPK_IN_GROUP].astype(jnp.int32).reshape(-1)
    onehot = (eid[:, None] == jnp.arange(N_EXPERTS)[None, :]).astype(jnp.int32)
    before = jnp.cumsum(onehot, axis=0) - onehot
    rank = jnp.sum(before * onehot, axis=1)
    counts = jnp.sum(onehot, axis=0)
    tiles = (counts + tm - 1) // tm
    tile_end = jnp.cumsum(tiles)
    tile_start = tile_end - tiles
    dst = tile_start[eid] * tm + rank
    n_tiles = (t * TOPK_IN_GROUP) // tm + N_EXPERTS
    tile_ids = jnp.arange(n_tiles)
    tile_expert = jnp.sum((tile_ids[:, None] >= tile_end[None, :]).astype(jnp.int32), axis=1)
    tile_expert = jnp.minimum(tile_expert, N_EXPERTS - 1)
    n_used = tile_end[-1:].astype(jnp.int32)
    dst = dst.astype(jnp.int32)
    tok = jnp.arange(t * TOPK_IN_GROUP, dtype=jnp.int32) // TOPK_IN_GROUP
    src_tok = jnp.zeros((n_tiles * tm,), jnp.int32).at[dst].set(tok)
    row_batch = (src_tok // seq).reshape(-1, 1)
    return dst, tile_expert.astype(jnp.int32), src_tok, n_used, row_batch


def _moe_layer(x, x_tm, scale, shift, gate, w_group, b_group, w_router, b_router, w1, w3, w2, layer,
               ln_g, ln_b, seq):
    route = _router(x, scale, shift, w_group, b_group, w_router, b_router, seq)
    dst, tile_expert, src_tok, n_used, row_batch = _sort_plan(route, seq)
    y_sorted = _experts(x_tm, scale, shift, tile_expert, src_tok, n_used, row_batch, w1, w3, w2,
                        layer)
    return _combine_ln(dst, y_sorted, route, x, gate, ln_g, ln_b, seq)


def _swa_layer(x, scale, shift, gate, tables, w_qkv, b_qkv, sinks, w_o, ln_g, ln_b, seq):
    qw = Q_HEADS * HEAD_DIM
    kvw = KV_HEADS * HEAD_DIM
    qkv = _mod_proj(x, scale, shift, w_qkv.astype(BF16), b_qkv, tables, seq,
                    rope_lo=qw // LANES, rope_hi=(qw + kvw) // LANES, tn=512)
    o = _band_attention(qkv, 0, qw // kvw, qw // kvw + 1, tables, seq, SWA_WINDOW, sinks=sinks)
    return _out_proj_ln([o], w_o.astype(BF16), x, gate, ln_g, ln_b, seq)


def _nsa_layer(x, scale, shift, gate, tables, w_in, pe_k, pe_v, phi_k1, phi_k2, phi_v1, phi_v2,
               w_o, ln_g, ln_b, seq):
    t, d = x.shape
    n_batch = t // seq
    qw = Q_HEADS * HEAD_DIM
    kvw = KV_HEADS * HEAD_DIM
    cols = [qw + i * kvw for i in range(7)]
    w_q, w_kc, w_vc, w_ks, w_vs, w_kw, w_vw, w_g = jnp.split(w_in, cols, axis=1)
    w_g = w_g.reshape(d, Q_HEADS, 3).transpose(0, 2, 1).reshape(d, 3 * Q_HEADS)
    w_g = jnp.pad(w_g, ((0, 0), (0, kvw - 3 * Q_HEADS)))
    w_all = jnp.concatenate([w_q, w_ks, w_kw, w_vs, w_vw, w_kc, w_vc, w_g], axis=1).astype(BF16)
    n_cols = w_all.shape[1]
    proj = _mod_proj(x, scale, shift, w_all, jnp.zeros((n_cols,), F32), tables, seq,
                     rope_lo=qw // LANES, rope_hi=(qw + 2 * kvw) // LANES, tn=768)
    kv_blk = qw // kvw
    k_s, v_s, k_c, v_c = (proj[:, qw + i * kvw: qw + (i + 1) * kvw] for i in (0, 2, 4, 5))
    gate_blk = (qw + 6 * kvw) // LANES

    kc = _compress(k_c, pe_k, phi_k1, phi_k2, seq)
    vc = _compress(v_c, pe_v, phi_v1, phi_v2, seq)
    o_cmp, selb = _cmp_attention(proj, kc, vc, gate_blk, seq, gate_col=0)

    heads = lambda a: a.reshape(n_batch, seq, KV_HEADS, HEAD_DIM).transpose(0, 2, 1, 3)
    onehot = (jnp.arange(seq)[:, None] // SEL_BLOCK == jnp.arange(HEAD_DIM)[None, :]).astype(BF16)
    onehot = jnp.broadcast_to(onehot, (n_batch, KV_HEADS, seq, HEAD_DIM))
    ks_h, vs_h = heads(k_s), heads(v_s)
    ka_even = jnp.concatenate([ks_h, onehot], axis=-1)
    ka_odd = jnp.concatenate([onehot, ks_h], axis=-1)
    vs2 = jnp.concatenate([vs_h, vs_h], axis=-1)
    o_sel = _sel_attention(proj, selb, ka_even, ka_odd, vs2, tables, gate_blk, seq, gate_col=Q_HEADS)
    o_win = _band_attention(proj, 0, kv_blk + 1, kv_blk + 3, tables, seq, NSA_WINDOW,
                            gate_blk=gate_blk, gate_col=2 * Q_HEADS)
    return _out_proj_ln([o_cmp, o_sel, o_win], w_o.astype(BF16), x, gate, ln_g, ln_b, seq)


def kernel(x, c, positions, w_ada, b_ada, swa_w_qkv, swa_b_qkv, swa_sinks, swa_w_o, nsa_w_in,
           nsa_pe_k, nsa_pe_v, nsa_phi_k1, nsa_phi_k2, nsa_phi_v1, nsa_phi_v2, nsa_w_o,
           moe_w_group, moe_b_group, moe_w_router, moe_b_router, moe_w1, moe_w3, moe_w2,
           ln_t_g, ln_t_b, ln_c_g, ln_c_b):
    n_batch, seq, d = x.shape
    depth = w_ada.shape[0]
    xt = x.reshape(n_batch * seq, d)
    mod = _adaln_mod(c, w_ada, b_ada)
    tables = _rope_tables(positions)
    for i in range(depth):
        sh_t, sc_t, g_t, sh_c, sc_c, g_c = (mod[i, :, k * d:(k + 1) * d] for k in range(6))
        j = i // 2
        if i % 2 == 0:
            xt, x_tm = _swa_layer(xt, sc_t, sh_t, g_t, tables, swa_w_qkv[j], swa_b_qkv[j], swa_sinks[j],
                            swa_w_o[j], ln_t_g[i], ln_t_b[i], seq)
        else:
            xt, x_tm = _nsa_layer(xt, sc_t, sh_t, g_t, tables, nsa_w_in[j], nsa_pe_k[j], nsa_pe_v[j],
                            nsa_phi_k1[j], nsa_phi_k2[j], nsa_phi_v1[j], nsa_phi_v2[j], nsa_w_o[j],
                            ln_t_g[i], ln_t_b[i], seq)
        xt = _moe_layer(xt, x_tm, sc_c, sh_c, g_c, moe_w_group[i], moe_b_group[i], moe_w_router[i],
                        moe_b_router[i], moe_w1, moe_w3, moe_w2, i, ln_c_g[i], ln_c_b[i], seq)
    return xt.reshape(n_batch, seq, d)
```

```python
import functools

import jax
import jax.numpy as jnp
from jax import lax
from jax.experimental import pallas as pl
from jax.experimental.pallas import tpu as pltpu

F32 = jnp.float32
BF16 = jnp.bfloat16

HEAD_DIM = 64
ROPE_DIM = HEAD_DIM // 4
ROPE_HALF = ROPE_DIM // 2
ROPE_THETA = 500000.0
Q_HEADS = 32
KV_HEADS = 4
GROUP = Q_HEADS // KV_HEADS
SWA_WINDOW = 128
NSA_WINDOW = 512
CMP_BLOCK = 32
CMP_STRIDE = 16
SEL_BLOCK = 64
SEL_TOPK = 16
FORCE_BONUS = 1e4
N_GROUPS = 4
EXPERTS_PER_GROUP = 4
N_EXPERTS = N_GROUPS * EXPERTS_PER_GROUP
TOPK_IN_GROUP = 2
DEPTH = 2
ALPHA = (2 * DEPTH) ** 0.25
LN_EPS = 1e-5
NEG_INF = -1e30
LOG2E = 1.4426950408889634

LANES = 128
ATTN_BLOCK = 128
VMEM_LIMIT = 56 * 1024 * 1024


def _cparams(*sem):
    return pltpu.CompilerParams(dimension_semantics=sem, vmem_limit_bytes=VMEM_LIMIT)


def _mod_kernel(cb_ref, w_ref, b_ref, o_ref, cs_ref, *, n_batch, tn):
    @pl.when((pl.program_id(0) == 0) & (pl.program_id(1) == 0))
    def _():
        c = cb_ref[...]
        cs_ref[...] = c * (1.0 / (1.0 + jnp.exp(-c)))

    for b in range(n_batch):
        cs = cs_ref[b]
        parts = []
        for g in range(tn // LANES):
            wg = w_ref[0, :, g * LANES:(g + 1) * LANES]
            parts.append(jnp.sum(wg * cs, axis=0, keepdims=True))
        o_ref[0, b:b + 1, :] = jnp.concatenate(parts, axis=-1) + b_ref[0]


def _adaln_mod(c, w_ada, b_ada):
    n_batch, d = c.shape
    depth, _, n = w_ada.shape
    tn = 512
    cb = jnp.broadcast_to(c[:, :, None], (n_batch, d, LANES))
    return pl.pallas_call(
        functools.partial(_mod_kernel, n_batch=n_batch, tn=tn),
        grid=(depth, n // tn),
        in_specs=[
            pl.BlockSpec((n_batch, d, LANES), lambda l, j: (0, 0, 0)),
            pl.BlockSpec((1, d, tn), lambda l, j: (l, 0, j)),
            pl.BlockSpec((1, 1, tn), lambda l, j: (l, 0, j)),
        ],
        out_specs=pl.BlockSpec((1, n_batch, tn), lambda l, j: (l, 0, j)),
        out_shape=jax.ShapeDtypeStruct((depth, n_batch, n), F32),
        scratch_shapes=[pltpu.VMEM((n_batch, d, LANES), F32)],
        compiler_params=_cparams("arbitrary", "arbitrary"),
    )(cb, w_ada, b_ada.reshape(depth, 1, n))


def _rope_table_kernel(pos_ref, inv_ref, c_ref, s1_ref, s2_ref):
    ang = pos_ref[...] * inv_ref[...]
    d = lax.broadcasted_iota(jnp.int32, ang.shape, 1) % HEAD_DIM
    cos = jnp.cos(ang)
    sin = jnp.sin(ang)
    c_ref[...] = jnp.where(d < ROPE_DIM, cos, 1.0)
    s1_ref[...] = jnp.where(d < ROPE_HALF, -sin, 0.0)
    s2_ref[...] = jnp.where((d >= ROPE_HALF) & (d < ROPE_DIM), sin, 0.0)


def _rope_tables(positions):
    t = positions.size
    tm = 1024
    inv_freq = ROPE_THETA ** (-jnp.arange(0, ROPE_DIM, 2, dtype=F32) / ROPE_DIM)
    lane = jnp.arange(LANES) % HEAD_DIM
    inv_lane = inv_freq[lane % ROPE_HALF].reshape(1, LANES)
    pos = positions.astype(F32).reshape(t, 1)
    spec = pl.BlockSpec((tm, LANES), lambda i: (i, 0))
    return pl.pallas_call(
        _rope_table_kernel,
        grid=(t // tm,),
        in_specs=[pl.BlockSpec((tm, 1), lambda i: (i, 0)),
                  pl.BlockSpec((1, LANES), lambda i: (0, 0))],
        out_specs=[spec, spec, spec],
        out_shape=[jax.ShapeDtypeStruct((t, LANES), F32)] * 3,
        compiler_params=_cparams("arbitrary"),
    )(pos, inv_lane)


def _apply_rope(x, c, s1, s2):
    reps = x.shape[-1] // LANES
    if reps > 1:
        c = jnp.concatenate([c] * reps, axis=-1)
        s1 = jnp.concatenate([s1] * reps, axis=-1)
        s2 = jnp.concatenate([s2] * reps, axis=-1)
    n = x.shape[-1]
    up = pltpu.roll(x, n - ROPE_HALF, 1)
    down = pltpu.roll(x, ROPE_HALF, 1)
    return x * c + up * s1 + down * s2


def _proj_kernel(x_ref, sc_ref, sh_ref, w_ref, b_ref, c_ref, s1_ref, s2_ref, o_ref, h_ref,
                 *, rope_lo, rope_hi, n_tiles):
    j = pl.program_id(1)
    groups = o_ref.shape[1] // LANES

    @pl.when(j == 0)
    def _():
        h_ref[...] = (x_ref[...] * (1.0 + sc_ref[0]) + sh_ref[0]).astype(BF16)

    acc = jnp.dot(h_ref[...], w_ref[...], preferred_element_type=F32) + b_ref[...]

    def write(pattern):
        for g in range(groups):
            piece = acc[:, g * LANES:(g + 1) * LANES]
            if pattern[g]:
                piece = _apply_rope(piece, c_ref[...], s1_ref[...], s2_ref[...])
            o_ref[:, g * LANES:(g + 1) * LANES] = piece.astype(o_ref.dtype)

    patterns = {}
    for tile in range(n_tiles):
        pattern = tuple(rope_lo <= tile * groups + g < rope_hi for g in range(groups))
        patterns.setdefault(pattern, []).append(tile)
    for pattern, tiles in patterns.items():
        cond = j == tiles[0]
        for tile in tiles[1:]:
            cond = cond | (j == tile)
        pl.when(cond)(functools.partial(write, pattern))


def _mod_proj(x, scale, shift, w, bias, tables, seq, *, rope_lo=0, rope_hi=0, tn=256, tm=1024,
              out_dtype=BF16):
    t, d = x.shape
    n = w.shape[1]
    n_batch = scale.shape[0]
    tm = min(tm, seq)
    per_batch = seq // tm
    c_tab, s1_tab, s2_tab = tables
    vec = pl.BlockSpec((1, 1, d), lambda i, j: (i // per_batch, 0, 0))
    tab = pl.BlockSpec((tm, LANES), lambda i, j: (i, 0))
    return pl.pallas_call(
        functools.partial(_proj_kernel, rope_lo=rope_lo, rope_hi=rope_hi, n_tiles=n // tn),
        grid=(t // tm, n // tn),
        in_specs=[
            pl.BlockSpec((tm, d), lambda i, j: (i, 0)),
            vec, vec,
            pl.BlockSpec((d, tn), lambda i, j: (0, j)),
            pl.BlockSpec((1, tn), lambda i, j: (0, j)),
            tab, tab, tab,
        ],
        out_specs=pl.BlockSpec((tm, tn), lambda i, j: (i, j)),
        out_shape=jax.ShapeDtypeStruct((t, n), out_dtype),
        scratch_shapes=[pltpu.VMEM((tm, d), BF16)],
        compiler_params=_cparams("arbitrary", "arbitrary"),
    )(x, scale.reshape(n_batch, 1, d), shift.reshape(n_batch, 1, d), w, bias.reshape(1, n),
      c_tab, s1_tab, s2_tab)


def _scaled_q(q_ref, tables=None):
    q = q_ref[...].astype(F32)
    if tables is not None:
        q = _apply_rope(q, *(t[...] for t in tables))
    return q * (HEAD_DIM ** -0.5 * LOG2E)


def _split_even_odd(q):
    even = (lax.broadcasted_iota(jnp.int32, q.shape, 1) % LANES) < HEAD_DIM
    return jnp.where(even, q, 0.0).astype(BF16), jnp.where(even, 0.0, q).astype(BF16)


def _stack_heads(q_even, q_odd, first_head):
    parts = []
    for g in range(GROUP):
        head = first_head + g
        src = q_even if head % 2 == 0 else q_odd
        parts.append(src[:, (head // 2) * LANES:(head // 2 + 1) * LANES])
    return jnp.concatenate(parts, axis=0)


def _twice(x):
    return jnp.concatenate([x, x], axis=-1)


def _merge_pairs(o, rows, scale):
    even = lax.broadcasted_iota(jnp.int32, (rows, LANES), 1) < HEAD_DIM
    out = []
    for g in range(0, GROUP, 2):
        a = o[g * rows:(g + 1) * rows] * scale[g]
        b = o[(g + 1) * rows:(g + 2) * rows] * scale[g + 1]
        out.append(jnp.where(even, a, b))
    return jnp.concatenate(out, axis=-1)


def _sigmoid(x):
    return 1.0 / (1.0 + jnp.exp(-x))


def _band_attn_kernel(*refs, window, use_sinks, use_gate, gate_col):
    it = iter(refs)
    q_ref, k_ref, v_ref, c_ref, s1_ref, s2_ref = (next(it) for _ in range(6))
    sink_ref = next(it) if use_sinks else None
    gate_ref = next(it) if use_gate else None
    o_ref = next(it)

    n = pl.program_id(1)
    rows = ATTN_BLOCK
    halo = -(-window // rows) * rows
    span = rows + halo
    start = jnp.maximum(n * rows - halo, 0)
    start = pl.multiple_of(start, rows)
    q_even, q_odd = _split_even_odd(_scaled_q(q_ref, (c_ref, s1_ref, s2_ref)))
    k = k_ref[pl.ds(start, span), :]
    v = v_ref[pl.ds(start, span), :]
    qpos = n * rows + lax.broadcasted_iota(jnp.int32, (rows, span), 0)
    kpos = start + lax.broadcasted_iota(jnp.int32, (rows, span), 1)
    rel = qpos - kpos
    mask = (rel >= 0) & (rel < window)
    if use_gate:
        gate = _sigmoid(gate_ref[...].astype(F32))

    for j in range(KV_HEADS):
        k2 = _twice(k[:, j * HEAD_DIM:(j + 1) * HEAD_DIM])
        v2 = _twice(v[:, j * HEAD_DIM:(j + 1) * HEAD_DIM])
        q8 = _stack_heads(q_even, q_odd, j * GROUP)
        s = lax.dot_general(q8, k2, (((1,), (1,)), ((), ())), preferred_element_type=F32)
        ps, scale = [], []
        for g in range(GROUP):
            sg = jnp.where(mask, s[g * rows:(g + 1) * rows], NEG_INF)
            m = jnp.max(sg, axis=-1, keepdims=True)
            if use_sinks:
                sink = sink_ref[j * GROUP + g] * LOG2E
                m = jnp.maximum(m, sink)
            e = jnp.exp2(sg - m)
            den = jnp.sum(e, axis=-1, keepdims=True)
            if use_sinks:
                den = den + jnp.exp2(sink - m)
            inv = 1.0 / den
            if use_gate:
                col = gate_col + j * GROUP + g
                inv = inv * gate[:, col:col + 1]
            ps.append(e.astype(BF16))
            scale.append(inv)
        o = jnp.dot(jnp.concatenate(ps, axis=0), v2, preferred_element_type=F32)
        o_ref[:, j * GROUP * HEAD_DIM:(j + 1) * GROUP * HEAD_DIM] = (
            _merge_pairs(o, rows, scale).astype(o_ref.dtype))


def _band_attention(proj, q_blk, k_blk, v_blk, tables, seq, window, sinks=None, gate_blk=None,
                    gate_col=0):
    t = proj.shape[0]
    n_batch = t // seq
    nb = seq // ATTN_BLOCK
    qw = Q_HEADS * HEAD_DIM
    kvw = KV_HEADS * HEAD_DIM
    row = lambda w, c: pl.BlockSpec((ATTN_BLOCK, w), lambda b, i: (b * nb + i, c))
    res = lambda c: pl.BlockSpec((seq, kvw), lambda b, i: (b, c))
    in_specs = [row(qw, q_blk), res(k_blk), res(v_blk), row(LANES, 0), row(LANES, 0), row(LANES, 0)]
    args = [proj, proj, proj, *tables]
    if sinks is not None:
        in_specs.append(pl.BlockSpec(memory_space=pltpu.SMEM))
        args.append(sinks)
    if gate_blk is not None:
        in_specs.append(row(LANES, gate_blk))
        args.append(proj)
    return pl.pallas_call(
        functools.partial(_band_attn_kernel, window=window, use_sinks=sinks is not None,
                          use_gate=gate_blk is not None, gate_col=gate_col),
        grid=(n_batch, nb),
        in_specs=in_specs,
        out_specs=row(qw, 0),
        out_shape=jax.ShapeDtypeStruct((t, qw), BF16),
        compiler_params=_cparams("arbitrary", "arbitrary"),
    )(*args)


def _layer_norm(y, g, b):
    mu = jnp.mean(y, axis=-1, keepdims=True)
    yc = y - mu
    var = jnp.mean(yc * yc, axis=-1, keepdims=True)
    return yc * lax.rsqrt(var + LN_EPS) * g + b


def _out_proj_kernel(*refs, n_parts):
    o_parts = refs[:n_parts]
    w_ref, x_ref, gate_ref, g_ref, b_ref, out_ref, out_tm_ref = refs[n_parts:]
    if n_parts == 1:
        o = o_parts[0][...]
    else:
        acc = o_parts[0][...].astype(F32)
        for r in o_parts[1:]:
            acc = acc + r[...].astype(F32)
        o = acc.astype(BF16)
    mix = jnp.dot(o, w_ref[...], preferred_element_type=F32)
    y = ALPHA * x_ref[...] + gate_ref[0] * mix
    res = _layer_norm(y, g_ref[...], b_ref[...])
    out_ref[...] = res
    _store_token_major(out_tm_ref, res)


def _out_proj_ln(o_parts, w_o, x, gate, ln_g, ln_b, seq, tm=256):
    t, d = x.shape
    n = d // LANES
    n_batch = gate.shape[0]
    per_batch = seq // tm
    k = w_o.shape[0]
    row_o = pl.BlockSpec((tm, k), lambda i: (i, 0))
    row_x = pl.BlockSpec((tm, d), lambda i: (i, 0))
    vec = pl.BlockSpec((1, d), lambda i: (0, 0))
    return pl.pallas_call(
        functools.partial(_out_proj_kernel, n_parts=len(o_parts)),
        grid=(t // tm,),
        in_specs=[row_o] * len(o_parts) + [
            pl.BlockSpec((k, d), lambda i: (0, 0)),
            row_x,
            pl.BlockSpec((1, 1, d), lambda i: (i // per_batch, 0, 0)),
            vec, vec],
        out_specs=[row_x, pl.BlockSpec((tm * n, LANES), lambda i: (i, 0))],
        out_shape=[jax.ShapeDtypeStruct((t, d), F32), jax.ShapeDtypeStruct((t * n, LANES), F32)],
        compiler_params=_cparams("arbitrary"),
    )(*o_parts, w_o, x, gate.reshape(n_batch, 1, d), ln_g.reshape(1, d), ln_b.reshape(1, d))


def _compress_kernel(x_ref, pe_ref, w1_ref, w1a_ref, w1b_ref, w2_ref, o_ref):
    x = x_ref[0]
    a = jnp.dot(x, w1a_ref[...], preferred_element_type=F32)
    b = jnp.dot(x, w1b_ref[...], preferred_element_type=F32)
    nc = a.shape[0]
    b_next = pltpu.roll(b, nc - 1, 0)
    pe_term = jnp.dot(pe_ref[...], w1_ref[...], preferred_element_type=F32)[0:1]
    hid = a + b_next + jnp.concatenate([pe_term] * KV_HEADS, axis=-1)
    hid = hid * (1.0 / (1.0 + jnp.exp(-hid)))
    o_ref[0] = jnp.dot(hid.astype(BF16), w2_ref[...], preferred_element_type=F32).astype(o_ref.dtype)


def _block_diag_heads(w):
    p, d, n = w.shape
    eye = jnp.eye(KV_HEADS, dtype=w.dtype)
    big = w[:, None, :, None, :] * eye[None, :, None, :, None]
    return big.reshape(p * KV_HEADS * d, KV_HEADS * n)


def _compress(xc, pe, w1, w2, seq):
    t, kvw = xc.shape
    n_batch = t // seq
    nch = seq // CMP_STRIDE
    hidden = w1.shape[1]
    x = xc.reshape(n_batch, nch, CMP_STRIDE * kvw)
    w1r = w1.reshape(CMP_BLOCK, HEAD_DIM, hidden)
    w1a = _block_diag_heads(w1r[:CMP_STRIDE]).astype(BF16)
    w1b = _block_diag_heads(w1r[CMP_STRIDE:]).astype(BF16)
    eye = jnp.eye(KV_HEADS, dtype=w2.dtype)
    w2d = (w2[None, :, None, :] * eye[:, None, :, None]).reshape(KV_HEADS * hidden, kvw).astype(BF16)
    pe8 = jnp.broadcast_to(pe.reshape(1, CMP_BLOCK * HEAD_DIM), (8, CMP_BLOCK * HEAD_DIM)).astype(BF16)
    full = lambda a: pl.BlockSpec(a.shape, lambda b: (0,) * a.ndim)
    w1b16 = w1.astype(BF16)
    return pl.pallas_call(
        _compress_kernel,
        grid=(n_batch,),
        in_specs=[pl.BlockSpec((1, nch, CMP_STRIDE * kvw), lambda b: (b, 0, 0)),
                  full(pe8), full(w1b16), full(w1a), full(w1b), full(w2d)],
        out_specs=pl.BlockSpec((1, nch, kvw), lambda b: (b, 0, 0)),
        out_shape=jax.ShapeDtypeStruct((n_batch, nch, kvw), BF16),
        compiler_params=_cparams("arbitrary"),
    )(x, pe8, w1b16, w1a, w1b, w2d)


def _topk_bias(score, top_k):
    nblk, rows = score.shape
    sub = lax.broadcasted_iota(jnp.int32, (8, rows), 0)
    chunks = [score[c * 8:(c + 1) * 8] for c in range(nblk // 8)]
    counts = [jnp.zeros((8, rows), F32) for _ in chunks]
    for sp in range(nblk):
        row = score[sp:sp + 1]
        for c, chunk in enumerate(chunks):
            ge = jnp.where(row >= chunk, 1.0, 0.0)
            gt = jnp.where(row > chunk, 1.0, 0.0)
            if sp < c * 8:
                beats = ge
            elif sp >= (c + 1) * 8:
                beats = gt
            else:
                beats = jnp.where(sub > sp - c * 8, ge, gt)
            counts[c] = counts[c] + beats
    return jnp.concatenate([jnp.where(cnt < top_k, 0.0, NEG_INF) for cnt in counts], axis=0)


def _cmp_attn_kernel(q_ref, kc_ref, vc_ref, ovt_ref, gate_ref, o_ref, sel_ref, *, gate_col, top_k):
    n = pl.program_id(1)
    rows = ATTN_BLOCK
    nc = kc_ref.shape[1]
    nblk = ovt_ref.shape[0]
    q_even, q_odd = _split_even_odd(_scaled_q(q_ref))
    kc = kc_ref[0]
    vc = vc_ref[0]
    t = n * rows + lax.broadcasted_iota(jnp.int32, (rows, nc), 0)
    cmp_end = lax.broadcasted_iota(jnp.int32, (rows, nc), 1) * CMP_STRIDE + (CMP_BLOCK - 1)
    valid = cmp_end <= t
    has_valid = n * rows + lax.broadcasted_iota(jnp.int32, (rows, 1), 0) >= CMP_BLOCK - 1
    gate = _sigmoid(gate_ref[...].astype(F32))

    ts = n * rows + lax.broadcasted_iota(jnp.int32, (nblk, rows), 1)
    blk = lax.broadcasted_iota(jnp.int32, (nblk, rows), 0)
    cur = lax.shift_right_arithmetic(ts, SEL_BLOCK.bit_length() - 1)
    causal = blk * SEL_BLOCK <= ts
    forced = (blk == 0) | (blk == cur) | (blk == cur - 1)
    bonus = jnp.where(forced, FORCE_BONUS, 0.0)

    for j in range(KV_HEADS):
        k2 = _twice(kc[:, j * HEAD_DIM:(j + 1) * HEAD_DIM])
        v2 = _twice(vc[:, j * HEAD_DIM:(j + 1) * HEAD_DIM])
        q8 = _stack_heads(q_even, q_odd, j * GROUP)
        s = lax.dot_general(q8, k2, (((1,), (1,)), ((), ())), preferred_element_type=F32)
        ps, scale, psum = [], [], None
        for g in range(GROUP):
            sg = jnp.where(valid, s[g * rows:(g + 1) * rows], NEG_INF)
            m = jnp.max(sg, axis=-1, keepdims=True)
            e = jnp.exp2(sg - m)
            den = jnp.sum(e, axis=-1, keepdims=True)
            inv = jnp.where(has_valid, 1.0 / den, 0.0)
            pn = e * inv
            psum = pn if psum is None else psum + pn
            col = gate_col + j * GROUP + g
            ps.append(e.astype(BF16))
            scale.append(inv * gate[:, col:col + 1])
        o = jnp.dot(jnp.concatenate(ps, axis=0), v2, preferred_element_type=F32)
        o_ref[:, j * GROUP * HEAD_DIM:(j + 1) * GROUP * HEAD_DIM] = (
            _merge_pairs(o, rows, scale).astype(o_ref.dtype))

        imp_t = lax.dot_general(ovt_ref[...], psum, (((1,), (1,)), ((), ())),
                                preferred_element_type=F32,
                                precision=lax.Precision.HIGHEST)
        score = jnp.where(causal, imp_t + bonus, NEG_INF)
        bias_t = _topk_bias(score, top_k)
        sel_ref[0, j] = jnp.concatenate([bias_t, bias_t], axis=0).T.astype(sel_ref.dtype)


def _cmp_attention(proj, kc, vc, gate_blk, seq, gate_col):
    t = proj.shape[0]
    n_batch = t // seq
    nb = seq // ATTN_BLOCK
    nc = kc.shape[1]
    nsel = seq // SEL_BLOCK
    assert nsel <= HEAD_DIM
    qw = Q_HEADS * HEAD_DIM
    top_k = min(SEL_TOPK, nsel)
    c0 = jnp.arange(nc)[None, :] * CMP_STRIDE
    s0 = jnp.arange(HEAD_DIM)[:, None] * SEL_BLOCK
    ov = jnp.clip(jnp.minimum(c0 + CMP_BLOCK, s0 + SEL_BLOCK) - jnp.maximum(c0, s0), 0)
    overlap_t = ov.astype(F32) / CMP_BLOCK
    row = lambda w, c: pl.BlockSpec((ATTN_BLOCK, w), lambda b, i: (b * nb + i, c))
    res = pl.BlockSpec((1, nc, kc.shape[2]), lambda b, i: (b, 0, 0))
    return pl.pallas_call(
        functools.partial(_cmp_attn_kernel, gate_col=gate_col, top_k=top_k),
        grid=(n_batch, nb),
        in_specs=[row(qw, 0), res, res,
                  pl.BlockSpec((HEAD_DIM, nc), lambda b, i: (0, 0)), row(LANES, gate_blk)],
        out_specs=[row(qw, 0),
                   pl.BlockSpec((1, KV_HEADS, ATTN_BLOCK, LANES), lambda b, i: (b, 0, i, 0))],
        out_shape=[jax.ShapeDtypeStruct((t, qw), BF16),
                   jax.ShapeDtypeStruct((n_batch, KV_HEADS, seq, LANES), BF16)],
        compiler_params=_cparams("arbitrary", "arbitrary"),
    )(proj, kc, vc, overlap_t, proj)


SEL_TILE = 512


def _sel_attn_kernel(q_ref, sel_ref, kae_ref, kao_ref, v_ref, c_ref, s1_ref, s2_ref, gate_ref, o_ref,
                     m_ref, l_ref, acc_ref, *, gate_col):
    j = pl.program_id(1)
    n = pl.program_id(2)
    rows = ATTN_BLOCK
    tk = SEL_TILE
    pairs = GROUP // 2
    q = _scaled_q(q_ref, (c_ref, s1_ref, s2_ref))
    bias2 = sel_ref[0, 0].astype(F32)
    low = lax.broadcasted_iota(jnp.int32, (rows, LANES), 1) < HEAD_DIM
    lhs = []
    for parity in range(2):
        parts = []
        for i in range(pairs):
            slab = q[:, i * LANES:(i + 1) * LANES]
            parts.append(jnp.where(low, slab, bias2) if parity == 0 else jnp.where(low, bias2, slab))
        lhs.append(jnp.concatenate(parts, axis=0).astype(BF16))
    ka_refs = (kae_ref, kao_ref)

    m_ref[...] = jnp.full(m_ref.shape, NEG_INF, F32)
    l_ref[...] = jnp.zeros(l_ref.shape, F32)
    acc_ref[...] = jnp.zeros(acc_ref.shape, F32)

    def tile(kt, mask_bias):
        off = pl.multiple_of(kt * tk, tk)
        v2 = v_ref[0, 0, pl.ds(off, tk), :]
        scores = []
        for parity in range(2):
            ka = ka_refs[parity][0, 0, pl.ds(off, tk), :]
            s = lax.dot_general(lhs[parity], ka, (((1,), (1,)), ((), ())),
                                preferred_element_type=F32)
            scores.append(s if mask_bias is None else s + mask_bias)
        probs, alphas = [], []
        for parity in range(2):
            s = scores[parity]
            chunks = [s[:, c * LANES:(c + 1) * LANES] for c in range(tk // LANES)]
            rmax = chunks[0]
            for ch in chunks[1:]:
                rmax = jnp.maximum(rmax, ch)
            m_old = m_ref[parity]
            m_new = jnp.maximum(m_old, jnp.max(rmax, axis=-1, keepdims=True))
            alpha = jnp.exp2(m_old - m_new)
            ps = [jnp.exp2(ch - m_new) for ch in chunks]
            lsum = ps[0]
            for x in ps[1:]:
                lsum = lsum + x
            l_ref[parity] = l_ref[parity] * alpha + lsum
            m_ref[parity] = m_new
            probs.append(jnp.concatenate([x.astype(BF16) for x in ps], axis=-1))
            alphas.append(alpha)
        for parity in range(2):
            acc_ref[parity] = acc_ref[parity] * alphas[parity] + jnp.dot(
                probs[parity], v2, preferred_element_type=F32)

    def body(kt, carry):
        tile(kt, None)
        return carry

    n_full = (n * rows) // tk
    lax.fori_loop(0, n_full, body, 0)
    qpos = n * rows + lax.broadcasted_iota(jnp.int32, (rows, tk), 0)
    kpos = n_full * tk + lax.broadcasted_iota(jnp.int32, (rows, tk), 1)
    causal = jnp.where(kpos <= qpos, 0.0, NEG_INF)
    tile(n_full, jnp.concatenate([causal] * pairs, axis=0))

    gate = _sigmoid(gate_ref[...].astype(F32))
    lane = lax.broadcasted_iota(jnp.int32, gate.shape, 1)
    outs = []
    for i in range(pairs):
        scaled = []
        for parity in range(2):
            col = gate_col + j * GROUP + 2 * i + parity
            gcol = jnp.sum(jnp.where(lane == col, gate, 0.0), axis=-1, keepdims=True)
            l = jnp.sum(l_ref[parity, i * rows:(i + 1) * rows], axis=-1, keepdims=True)
            scaled.append(acc_ref[parity, i * rows:(i + 1) * rows] * (gcol / l))
        outs.append(jnp.where(low, scaled[0], scaled[1]))
    o_ref[...] = jnp.concatenate(outs, axis=-1).astype(o_ref.dtype)


def _sel_attention(proj, selb, ka_even, ka_odd, v2, tables, gate_blk, seq, gate_col):
    t = proj.shape[0]
    n_batch = t // seq
    nb = seq // ATTN_BLOCK
    gw = GROUP * HEAD_DIM
    pairs = GROUP // 2
    row = lambda c: pl.BlockSpec((ATTN_BLOCK, LANES), lambda b, j, i: (b * nb + i, c))
    res = pl.BlockSpec((1, 1, seq, LANES), lambda b, j, i: (b, j, 0, 0))
    acc = pltpu.VMEM((2, pairs * ATTN_BLOCK, LANES), F32)
    return pl.pallas_call(
        functools.partial(_sel_attn_kernel, gate_col=gate_col),
        grid=(n_batch, KV_HEADS, nb),
        in_specs=[
            pl.BlockSpec((ATTN_BLOCK, gw), lambda b, j, i: (b * nb + i, j)),
            pl.BlockSpec((1, 1, ATTN_BLOCK, LANES), lambda b, j, i: (b, j, i, 0)),
            res, res, res, row(0), row(0), row(0), row(gate_blk)],
        out_specs=pl.BlockSpec((ATTN_BLOCK, gw), lambda b, j, i: (b * nb + i, j)),
        out_shape=jax.ShapeDtypeStruct((t, Q_HEADS * HEAD_DIM), BF16),
        scratch_shapes=[acc, acc, acc],
        compiler_params=_cparams("arbitrary", "arbitrary", "arbitrary"),
    )(proj, selb, ka_even, ka_odd, v2, *tables, proj)


def _router_kernel(x_ref, sc_ref, sh_ref, w_ref, b_ref, o_ref):
    h = x_ref[...] * (1.0 + sc_ref[0]) + sh_ref[0]
    w = w_ref[...]
    h_hi = h.astype(BF16)
    h_lo = (h - h_hi.astype(F32)).astype(BF16)
    w_hi = w.astype(BF16)
    w_lo = (w - w_hi.astype(F32)).astype(BF16)
    logits = (jnp.dot(h_hi, w_hi, preferred_element_type=F32)
              + jnp.dot(h_lo, w_hi, preferred_element_type=F32)
              + jnp.dot(h_hi, w_lo, preferred_element_type=F32)) + b_ref[...]
    lane = lax.broadcasted_iota(jnp.int32, logits.shape, 1)
    big = 1 << 20

    gmask = lane < N_GROUPS
    lg = jnp.where(gmask, logits, -jnp.inf)
    eg = jnp.exp(lg - jnp.max(lg, axis=-1, keepdims=True))
    pg = eg / jnp.sum(eg, axis=-1, keepdims=True)
    g_prob = jnp.max(pg, axis=-1, keepdims=True)
    g_idx = jnp.min(jnp.where((pg == g_prob) & gmask, lane, big), axis=-1, keepdims=True)

    lo = N_GROUPS + g_idx * EXPERTS_PER_GROUP
    emask = (lane >= lo) & (lane < lo + EXPERTS_PER_GROUP)
    le = jnp.where(emask, logits, -jnp.inf)
    ee = jnp.exp(le - jnp.max(le, axis=-1, keepdims=True))
    pe = jnp.where(emask, ee / jnp.sum(ee, axis=-1, keepdims=True), -1.0)
    p1 = jnp.max(pe, axis=-1, keepdims=True)
    i1 = jnp.min(jnp.where(pe == p1, lane, big), axis=-1, keepdims=True)
    pe2 = jnp.where(lane == i1, -1.0, pe)
    p2 = jnp.max(pe2, axis=-1, keepdims=True)
    i2 = jnp.min(jnp.where(pe2 == p2, lane, big), axis=-1, keepdims=True)
    tot = p1 + p2
    w1 = g_prob * (p1 / tot)
    w2 = g_prob * (p2 / tot)
    e1 = (i1 - N_GROUPS).astype(F32)
    e2 = (i2 - N_GROUPS).astype(F32)
    o_ref[...] = jnp.where(lane == 0, e1, jnp.where(lane == 1, e2,
                           jnp.where(lane == 2, w1, jnp.where(lane == 3, w2, 0.0))))


def _router(x, scale, shift, w_group, b_group, w_router, b_router, seq, tm=512):
    t, d = x.shape
    n_batch = scale.shape[0]
    per_batch = seq // tm
    w = jnp.zeros((d, LANES), F32).at[:, :N_GROUPS].set(w_group)
    w = w.at[:, N_GROUPS:N_GROUPS + N_EXPERTS].set(w_router)
    b = jnp.zeros((1, LANES), F32).at[0, :N_GROUPS].set(b_group)
    b = b.at[0, N_GROUPS:N_GROUPS + N_EXPERTS].set(b_router)
    vec = pl.BlockSpec((1, 1, d), lambda i: (i // per_batch, 0, 0))
    return pl.pallas_call(
        _router_kernel,
        grid=(t // tm,),
        in_specs=[pl.BlockSpec((tm, d), lambda i: (i, 0)), vec, vec,
                  pl.BlockSpec((d, LANES), lambda i: (0, 0)),
                  pl.BlockSpec((1, LANES), lambda i: (0, 0))],
        out_specs=pl.BlockSpec((tm, LANES), lambda i: (i, 0)),
        out_shape=jax.ShapeDtypeStruct((t, LANES), F32),
        compiler_params=_cparams("arbitrary"),
    )(x, scale.reshape(n_batch, 1, d), shift.reshape(n_batch, 1, d), w, b)


MOE_TILE = 256


def _store_token_major(ref, val):
    rows, n = val.shape[0], val.shape[1] // LANES
    for a in range(n):
        ref[pl.ds(a, rows, stride=n), :] = val[:, a * LANES:(a + 1) * LANES]


def _load_token_major(ref, rows, n):
    return jnp.concatenate([ref[pl.ds(a, rows, stride=n), :] for a in range(n)], axis=-1)


def _expert_kernel(te_ref, src_ref, nt_ref, x_hbm, rowb_ref, sc_ref, sh_ref, w1_ref, w3_ref, w2_ref,
                   o_ref, xbuf, w1b, w3b, w2b, sem, *, n_batch):
    i = pl.program_id(0)
    n_used = nt_ref[0]
    tm = MOE_TILE
    n = sc_ref.shape[-1] // LANES

    def row_copy(tile_idx, slot, r):
        tok = src_ref[tile_idx * tm + r]
        return pltpu.make_async_copy(x_hbm.at[pl.ds(pl.multiple_of(tok * n, n), n), :],
                                     xbuf.at[slot, pl.ds(pl.multiple_of(r * n, n), n), :],
                                     sem.at[slot])

    def start_rows(tile_idx, slot, lo, hi):
        def body(r, c):
            row_copy(tile_idx, slot, r).start()
            return c
        lax.fori_loop(lo, hi, body, 0, unroll=8)

    def wait_tile(slot):
        pltpu.make_async_copy(x_hbm.at[pl.ds(0, tm * n), :], xbuf.at[slot], sem.at[slot]).wait()

    @pl.when((i == 0) & (n_used > 0))
    def _():
        start_rows(0, 0, 0, tm)

    new_expert = (i == 0) | (te_ref[i] != te_ref[jnp.maximum(i - 1, 0)])

    @pl.when((i < n_used) & new_expert)
    def _():
        w1b[...] = w1_ref[0, 0].astype(BF16)
        w3b[...] = w3_ref[0, 0].astype(BF16)
        w2b[...] = w2_ref[0, 0].astype(BF16)

    @pl.when(i < n_used)
    def _():
        slot = i % 2
        wait_tile(slot)
        x = _load_token_major(xbuf.at[slot], tm, n)
        rowb = rowb_ref[...]
        h = x * (1.0 + sc_ref[0]) + sh_ref[0]
        for b in range(1, n_batch):
            h = jnp.where(rowb == b, x * (1.0 + sc_ref[b]) + sh_ref[b], h)
        hb = h.astype(BF16)

        nxt = jnp.minimum(i + 1, n_used - 1)
        nslot = (i + 1) % 2
        quarter = tm // 4

        def prefetch(k):
            for r in range(k * quarter, (k + 1) * quarter):
                row_copy(nxt, nslot, r).start(priority=r % 2)

        prefetch(0)
        a = jnp.dot(hb, w1b[...], preferred_element_type=F32)
        prefetch(1)
        g = jnp.dot(hb, w3b[...], preferred_element_type=F32)
        prefetch(2)
        he = (a * _sigmoid(a) * g).astype(BF16)
        y = jnp.dot(he, w2b[...], preferred_element_type=F32)
        prefetch(3)
        _store_token_major(o_ref, y)

        @pl.when(i + 1 >= n_used)
        def _():
            wait_tile(nslot)

    @pl.when(i >= n_used)
    def _():
        o_ref[...] = jnp.zeros(o_ref.shape, o_ref.dtype)


def _experts(x_tm, scale, shift, tile_expert, src_tok, n_used, row_batch, w1, w3, w2, layer):
    n_batch, d = scale.shape
    n = d // LANES
    n_tiles = tile_expert.shape[0]
    de = w1.shape[3]
    tm = MOE_TILE
    grid_spec = pltpu.PrefetchScalarGridSpec(
        num_scalar_prefetch=3,
        grid=(n_tiles,),
        in_specs=[
            pl.BlockSpec(memory_space=pl.ANY),
            pl.BlockSpec((tm, 1), lambda i, te, src, nt: (i, 0)),
            pl.BlockSpec((n_batch, 1, d), lambda i, te, src, nt: (0, 0, 0)),
            pl.BlockSpec((n_batch, 1, d), lambda i, te, src, nt: (0, 0, 0)),
            pl.BlockSpec((1, 1, d, de), lambda i, te, src, nt: (layer, te[i], 0, 0)),
            pl.BlockSpec((1, 1, d, de), lambda i, te, src, nt: (layer, te[i], 0, 0)),
            pl.BlockSpec((1, 1, de, d), lambda i, te, src, nt: (layer, te[i], 0, 0)),
        ],
        out_specs=pl.BlockSpec((tm * n, LANES), lambda i, te, src, nt: (i, 0)),
        scratch_shapes=[pltpu.VMEM((2, tm * n, LANES), F32),
                        pltpu.VMEM((d, de), BF16), pltpu.VMEM((d, de), BF16),
                        pltpu.VMEM((de, d), BF16), pltpu.SemaphoreType.DMA((2,))],
    )
    return pl.pallas_call(
        functools.partial(_expert_kernel, n_batch=n_batch),
        grid_spec=grid_spec,
        out_shape=jax.ShapeDtypeStruct((n_tiles * tm * n, LANES), F32),
        compiler_params=_cparams("arbitrary"),
    )(tile_expert, src_tok, n_used, x_tm, row_batch, scale.reshape(n_batch, 1, d),
      shift.reshape(n_batch, 1, d), w1, w3, w2)


COMBINE_TILE = 256


def _combine_kernel(dst_ref, y_hbm, route_ref, x_ref, gate_ref, g_ref, b_ref, out_ref, ybuf, sem):
    i = pl.program_id(0)
    n_steps = pl.num_programs(0)
    tm = COMBINE_TILE

    n = x_ref.shape[-1] // LANES

    def row_copy(step, slot, k, r):
        pos = dst_ref[(step * tm + r) * TOPK_IN_GROUP + k]
        return pltpu.make_async_copy(
            y_hbm.at[pl.ds(pl.multiple_of(pos * n, n), n), :],
            ybuf.at[slot, pl.ds(pl.multiple_of((k * tm + r) * n, n), n), :], sem.at[slot])

    def start_step(step, slot):
        def body(r, c):
            for k in range(TOPK_IN_GROUP):
                row_copy(step, slot, k, r).start(priority=k % 2)
            return c
        lax.fori_loop(0, tm, body, 0, unroll=4)

    def wait_step(slot):
        pltpu.make_async_copy(y_hbm.at[pl.ds(0, TOPK_IN_GROUP * tm * n), :], ybuf.at[slot],
                              sem.at[slot]).wait()

    @pl.when(i == 0)
    def _():
        start_step(0, 0)

    @pl.when(i + 1 < n_steps)
    def _():
        start_step(i + 1, (i + 1) % 2)

    slot = i % 2
    wait_step(slot)
    route = route_ref[...]
    lane = lax.broadcasted_iota(jnp.int32, route.shape, 1)
    mix = None
    for k in range(TOPK_IN_GROUP):
        wk = jnp.sum(jnp.where(lane == TOPK_IN_GROUP + k, route, 0.0), axis=-1, keepdims=True)
        term = wk * _load_token_major(ybuf.at[slot, pl.ds(k * tm * n, tm * n), :], tm, n)
        mix = term if mix is None else mix + term
    y = ALPHA * x_ref[...] + gate_ref[0] * mix
    out_ref[...] = _layer_norm(y, g_ref[...], b_ref[...])


def _combine_ln(dst, y_sorted, route, x, gate, ln_g, ln_b, seq):
    t, d = x.shape
    n_batch = gate.shape[0]
    tm = COMBINE_TILE
    per_batch = seq // tm
    row = lambda w: pl.BlockSpec((tm, w), lambda i, dst: (i, 0))
    vec = pl.BlockSpec((1, d), lambda i, dst: (0, 0))
    grid_spec = pltpu.PrefetchScalarGridSpec(
        num_scalar_prefetch=1,
        grid=(t // tm,),
        in_specs=[pl.BlockSpec(memory_space=pl.ANY), row(LANES), row(d),
                  pl.BlockSpec((1, 1, d), lambda i, dst: (i // per_batch, 0, 0)), vec, vec],
        out_specs=row(d),
        scratch_shapes=[pltpu.VMEM((2, TOPK_IN_GROUP * tm * (d // LANES), LANES), F32),
                        pltpu.SemaphoreType.DMA((2,))],
    )
    return pl.pallas_call(
        _combine_kernel,
        grid_spec=grid_spec,
        out_shape=jax.ShapeDtypeStruct((t, d), F32),
        compiler_params=_cparams("arbitrary"),
    )(dst, y_sorted, route, x, gate.reshape(n_batch, 1, d), ln_g.reshape(1, d), ln_b.reshape(1, d))


def _sort_plan(route, seq):
    t = route.shape[0]
    tm = MOE_TILE
    eid = route[:, :TOPK_IN_GROUP].astype(jnp.int32).reshape(-1)
    onehot = (eid[:, None] == jnp.arange(N_EXPERTS)[None, :]).astype(jnp.int32)
    before = jnp.cumsum(onehot, axis=0) - onehot
    rank = jnp.sum(before * onehot, axis=1)
    counts = jnp.sum(onehot, axis=0)
    tiles = (counts + tm - 1) // tm
    tile_end = jnp.cumsum(tiles)
    tile_start = tile_end - tiles
    dst = tile_start[eid] * tm + rank
    n_tiles = (t * TOPK_IN_GROUP) // tm + N_EXPERTS
    tile_ids = jnp.arange(n_tiles)
    tile_expert = jnp.sum((tile_ids[:, None] >= tile_end[None, :]).astype(jnp.int32), axis=1)
    tile_expert = jnp.minimum(tile_expert, N_EXPERTS - 1)
    n_used = tile_end[-1:].astype(jnp.int32)
    dst = dst.astype(jnp.int32)
    tok = jnp.arange(t * TOPK_IN_GROUP, dtype=jnp.int32) // TOPK_IN_GROUP
    src_tok = jnp.zeros((n_tiles * tm,), jnp.int32).at[dst].set(tok)
    row_batch = (src_tok // seq).reshape(-1, 1)
    return dst, tile_expert.astype(jnp.int32), src_tok, n_used, row_batch


def _moe_layer(x, x_tm, scale, shift, gate, w_group, b_group, w_router, b_router, w1, w3, w2, layer,
               ln_g, ln_b, seq):
    route = _router(x, scale, shift, w_group, b_group, w_router, b_router, seq)
    dst, tile_expert, src_tok, n_used, row_batch = _sort_plan(route, seq)
    y_sorted = _experts(x_tm, scale, shift, tile_expert, src_tok, n_used, row_batch, w1, w3, w2,
                        layer)
    return _combine_ln(dst, y_sorted, route, x, gate, ln_g, ln_b, seq)


def _swa_layer(x, scale, shift, gate, tables, w_qkv, b_qkv, sinks, w_o, ln_g, ln_b, seq):
    qw = Q_HEADS * HEAD_DIM
    kvw = KV_HEADS * HEAD_DIM
    qkv = _mod_proj(x, scale, shift, w_qkv.astype(BF16), b_qkv, tables, seq,
                    rope_lo=qw // LANES, rope_hi=(qw + kvw) // LANES, tn=512)
    o = _band_attention(qkv, 0, qw // kvw, qw // kvw + 1, tables, seq, SWA_WINDOW, sinks=sinks)
    return _out_proj_ln([o], w_o.astype(BF16), x, gate, ln_g, ln_b, seq)


def _nsa_layer(x, scale, shift, gate, tables, w_in, pe_k, pe_v, phi_k1, phi_k2, phi_v1, phi_v2,
               w_o, ln_g, ln_b, seq):
    t, d = x.shape
    n_batch = t // seq
    qw = Q_HEADS * HEAD_DIM
    kvw = KV_HEADS * HEAD_DIM
    cols = [qw + i * kvw for i in range(7)]
    w_q, w_kc, w_vc, w_ks, w_vs, w_kw, w_vw, w_g = jnp.split(w_in, cols, axis=1)
    w_g = w_g.reshape(d, Q_HEADS, 3).transpose(0, 2, 1).reshape(d, 3 * Q_HEADS)
    w_g = jnp.pad(w_g, ((0, 0), (0, kvw - 3 * Q_HEADS)))
    w_all = jnp.concatenate([w_q, w_ks, w_kw, w_vs, w_vw, w_kc, w_vc, w_g], axis=1).astype(BF16)
    n_cols = w_all.shape[1]
    proj = _mod_proj(x, scale, shift, w_all, jnp.zeros((n_cols,), F32), tables, seq,
                     rope_lo=qw // LANES, rope_hi=(qw + 2 * kvw) // LANES, tn=768)
    kv_blk = qw // kvw
    k_s, v_s, k_c, v_c = (proj[:, qw + i * kvw: qw + (i + 1) * kvw] for i in (0, 2, 4, 5))
    gate_blk = (qw + 6 * kvw) // LANES

    kc = _compress(k_c, pe_k, phi_k1, phi_k2, seq)
    vc = _compress(v_c, pe_v, phi_v1, phi_v2, seq)
    o_cmp, selb = _cmp_attention(proj, kc, vc, gate_blk, seq, gate_col=0)

    heads = lambda a: a.reshape(n_batch, seq, KV_HEADS, HEAD_DIM).transpose(0, 2, 1, 3)
    onehot = (jnp.arange(seq)[:, None] // SEL_BLOCK == jnp.arange(HEAD_DIM)[None, :]).astype(BF16)
    onehot = jnp.broadcast_to(onehot, (n_batch, KV_HEADS, seq, HEAD_DIM))
    ks_h, vs_h = heads(k_s), heads(v_s)
    ka_even = jnp.concatenate([ks_h, onehot], axis=-1)
    ka_odd = jnp.concatenate([onehot, ks_h], axis=-1)
    vs2 = jnp.concatenate([vs_h, vs_h], axis=-1)
    o_sel = _sel_attention(proj, selb, ka_even, ka_odd, vs2, tables, gate_blk, seq, gate_col=Q_HEADS)
    o_win = _band_attention(proj, 0, kv_blk + 1, kv_blk + 3, tables, seq, NSA_WINDOW,
                            gate_blk=gate_blk, gate_col=2 * Q_HEADS)
    return _out_proj_ln([o_cmp, o_sel, o_win], w_o.astype(BF16), x, gate, ln_g, ln_b, seq)


def kernel(x, c, positions, w_ada, b_ada, swa_w_qkv, swa_b_qkv, swa_sinks, swa_w_o, nsa_w_in,
           nsa_pe_k, nsa_pe_v, nsa_phi_k1, nsa_phi_k2, nsa_phi_v1, nsa_phi_v2, nsa_w_o,
           moe_w_group, moe_b_group, moe_w_router, moe_b_router, moe_w1, moe_w3, moe_w2,
           ln_t_g, ln_t_b, ln_c_g, ln_c_b):
    n_batch, seq, d = x.shape
    depth = w_ada.shape[0]
    xt = x.reshape(n_batch * seq, d)
    mod = _adaln_mod(c, w_ada, b_ada)
    tables = _rope_tables(positions)
    for i in range(depth):
        sh_t, sc_t, g_t, sh_c, sc_c, g_c = (mod[i, :, k * d:(k + 1) * d] for k in range(6))
        j = i // 2
        if i % 2 == 0:
            xt, x_tm = _swa_layer(xt, sc_t, sh_t, g_t, tables, swa_w_qkv[j], swa_b_qkv[j], swa_sinks[j],
                            swa_w_o[j], ln_t_g[i], ln_t_b[i], seq)
        else:
            xt, x_tm = _nsa_layer(xt, sc_t, sh_t, g_t, tables, nsa_w_in[j], nsa_pe_k[j], nsa_pe_v[j],
                            nsa_phi_k1[j], nsa_phi_k2[j], nsa_phi_v1[j], nsa_phi_v2[j], nsa_w_o[j],
                            ln_t_g[i], ln_t_b[i], seq)
        xt = _moe_layer(xt, x_tm, sc_c, sh_c, g_c, moe_w_group[i], moe_b_group[i], moe_w_router[i],
                        moe_b_router[i], moe_w1, moe_w3, moe_w2, i, ln_c_g[i], ln_c_b[i], seq)
    return xt.reshape(n_batch, seq, d)
```
